```python
import math
import jax, jax.numpy as jnp
from jax import lax
import numpy as np

D_MODEL = 2048
BATCH = 2
SEQ = 4096
DEPTH = 4
DEC_BATCH = 8
DEC_SEQ = 4
PAST_LEN = 16384
PAGE_SIZE = 128

N_HEADS = 8
HEAD_DIM = 64
KEY_DIM = 2 * HEAD_DIM
VAL_DIM = 2 * HEAD_DIM
Q_W = N_HEADS * KEY_DIM
K_W = N_HEADS * KEY_DIM
V_W = N_HEADS * VAL_DIM
ATTN_W = V_W
Q_BLOCK = 128
SG_WIDTH = 1024
SG_GROUPS = 8
SG_CH = SG_WIDTH // SG_GROUPS
CHUNK = 128
IN_COLS = Q_W + K_W + V_W + 2 * SG_WIDTH + 2 * D_MODEL
N_EGROUPS = 4
EXP_PER_GROUP = 8
N_EXPERTS = N_EGROUPS * EXP_PER_GROUP
TOP_K_IN_GROUP = 2
D_EXPERT = 512
ALPHA = (2 * DEPTH) ** 0.25
BETA = (8 * DEPTH) ** -0.25
LN_EPS = 1e-5
NEG = -1e30

kernel_name = "hybrid_diffattn_gmlp_hmoe_step"


def layer_norm(x, g, b):
    xf = x.astype(jnp.float32)
    mu = jnp.mean(xf, axis=-1, keepdims=True)
    var = jnp.mean(jnp.square(xf - mu), axis=-1, keepdims=True)
    y = (xf - mu) * lax.rsqrt(var + LN_EPS) * g.astype(jnp.float32) + b.astype(jnp.float32)
    return y.astype(x.dtype)


def lambda_init(layer_idx):
    return 0.8 - 0.6 * math.exp(-0.3 * layer_idx)


def mixer_inputs(x, w_in, b_in, sg_g, sg_b):
    B, T, _ = x.shape
    z = jnp.einsum('btd,dc->btc', x, w_in) + b_in
    q, k, v, sg, gates = jnp.split(z, [Q_W, Q_W + K_W, Q_W + K_W + V_W, Q_W + K_W + V_W + 2 * SG_WIDTH], axis=-1)
    q = q.reshape(B, T, N_HEADS, KEY_DIM)
    k = k.reshape(B, T, N_HEADS, KEY_DIM)
    v = v.reshape(B, T, N_HEADS, VAL_DIM)
    u, vs = jnp.split(jax.nn.gelu(sg), 2, axis=-1)
    vs = layer_norm(vs, sg_g, sg_b)
    ga, gb = jnp.split(jax.nn.sigmoid(gates), 2, axis=-1)
    return q, k, v, u, vs, ga, gb


def diff_attn_core(q, k, v, q_pos, k_pos, lam):
    B, Tq = q.shape[:2]
    Tk = k.shape[1]
    qm = q.reshape(B, Tq, N_HEADS, 2, HEAD_DIM)
    km = k.reshape(B, Tk, N_HEADS, 2, HEAD_DIM)
    s = jnp.einsum('bqhmd,bkhmd->bhmqk', qm, km).astype(jnp.float32) * (HEAD_DIM ** -0.5)
    s = jnp.where(k_pos[None, :] <= q_pos[:, None], s, NEG)
    p = jax.nn.softmax(s, axis=-1)
    a = (p[:, :, 0] - lam * p[:, :, 1]).astype(v.dtype)
    return jnp.einsum('bhqk,bkhd->bqhd', a, v)


def subln(o, g, lam_init):
    B, T = o.shape[:2]
    of = o.astype(jnp.float32)
    r = of * lax.rsqrt(jnp.mean(of * of, axis=-1, keepdims=True) + LN_EPS) * g.astype(jnp.float32)
    return (r * (1.0 - lam_init)).astype(o.dtype).reshape(B, T, ATTN_W)


def prompt_attention(q, k, v, lam):
    B, T = q.shape[:2]
    nb = T // Q_BLOCK
    qb = q.reshape(B, nb, Q_BLOCK, N_HEADS, KEY_DIM).transpose(1, 0, 2, 3, 4)
    pos = jnp.arange(T, dtype=jnp.int32).reshape(nb, Q_BLOCK)
    k_pos = jnp.arange(T, dtype=jnp.int32)
    o = lax.map(lambda a: diff_attn_core(a[0], k, v, a[1], k_pos, lam), (qb, pos))
    return o.transpose(1, 0, 2, 3, 4).reshape(B, T, N_HEADS, VAL_DIM)


def sample_attention(q, k_new, v_new, k_pool, v_pool, page_table, lam):
    DB, T = q.shape[:2]
    past_k = k_pool[page_table].reshape(DB, -1, N_HEADS, KEY_DIM)
    past_v = v_pool[page_table].reshape(DB, -1, N_HEADS, VAL_DIM)
    P = past_k.shape[1]
    k = jnp.concatenate([past_k, k_new], axis=1)
    v = jnp.concatenate([past_v, v_new], axis=1)
    q_pos = P + jnp.arange(T, dtype=jnp.int32)
    k_pos = jnp.arange(P + T, dtype=jnp.int32)
    return diff_attn_core(q, k, v, q_pos, k_pos, lam)


def spatial_gating(u, vs, w_sp, b_sp):
    B, T, _ = vs.shape
    Tp = -(-T // CHUNK) * CHUNK
    vp = jnp.pad(vs, ((0, 0), (0, Tp - T), (0, 0))).reshape(B, Tp // CHUNK, CHUNK, SG_GROUPS, SG_CH)
    w = w_sp * jnp.tril(jnp.ones((CHUNK, CHUNK), w_sp.dtype))
    mixed = jnp.einsum('gts,bcsgd->bctgd', w, vp) + b_sp.T[:, :, None]
    mixed = mixed.reshape(B, Tp, SG_WIDTH)[:, :T]
    return u * mixed


def hier_moe(h, w_rg, b_rg, w_re, b_re, w_gate, w_up, w_down):
    B, T, D = h.shape
    hf = h.reshape(-1, D)
    gl = (hf @ w_rg + b_rg).astype(jnp.float32)
    gp = jax.nn.softmax(gl, axis=-1)
    gi = jnp.argmax(gl, axis=-1)
    gw = jnp.take_along_axis(gp, gi[:, None], axis=1)
    el = (hf @ w_re + b_re).astype(jnp.float32).reshape(-1, N_EGROUPS, EXP_PER_GROUP)
    el = jnp.take_along_axis(el, gi[:, None, None], axis=1)[:, 0]
    tv, ti = lax.top_k(el, TOP_K_IN_GROUP)
    tw = jax.nn.softmax(tv, axis=-1) * gw
    ei = gi[:, None] * EXP_PER_GROUP + ti
    comb = jnp.sum(jax.nn.one_hot(ei, N_EXPERTS, dtype=jnp.float32) * tw[..., None], axis=1).astype(h.dtype)
    hg = jnp.einsum('nd,edf->nef', hf, w_gate)
    hu = jnp.einsum('nd,edf->nef', hf, w_up)
    act = jax.nn.silu(hg) * hu * comb[:, :, None]
    return jnp.einsum('nef,efd->nd', act, w_down).reshape(B, T, D)


def setup_inputs(seed: int = 0) -> dict:
    key = jax.random.key(seed)
    ks = jax.random.split(key, 32)
    nrm = lambda k, shape, s: jax.random.normal(k, shape, jnp.float32) * s
    n_pages = PAST_LEN // PAGE_SIZE
    n_used = DEC_BATCH * n_pages
    n_pool = n_used + n_used // 4
    page_table = jax.random.permutation(ks[4], n_pool)[:n_used].reshape(DEC_BATCH, n_pages).astype(jnp.int32)
    col_scale = jnp.ones((IN_COLS,), jnp.float32).at[Q_W + K_W:Q_W + K_W + V_W].set(BETA)
    w_in = nrm(ks[5], (DEPTH, D_MODEL, IN_COLS), D_MODEL ** -0.5) * col_scale
    return {
        "x_prompt": nrm(ks[0], (BATCH, SEQ, D_MODEL), 1.0),
        "x_sample": nrm(ks[1], (DEC_BATCH, DEC_SEQ, D_MODEL), 1.0),
        "cache_k": nrm(ks[2], (DEPTH, n_pool, PAGE_SIZE, N_HEADS, KEY_DIM), 1.0),
        "cache_v": nrm(ks[3], (DEPTH, n_pool, PAGE_SIZE, N_HEADS, VAL_DIM), 1.0),
        "page_table": page_table,
        "w_in": w_in,
        "b_in": nrm(ks[6], (DEPTH, IN_COLS), 0.01),
        "lam_q1": nrm(ks[7], (DEPTH, HEAD_DIM), 0.1),
        "lam_k1": nrm(ks[8], (DEPTH, HEAD_DIM), 0.1),
        "lam_q2": nrm(ks[9], (DEPTH, HEAD_DIM), 0.1),
        "lam_k2": nrm(ks[10], (DEPTH, HEAD_DIM), 0.1),
        "subln_g": 1.0 + nrm(ks[11], (DEPTH, VAL_DIM), 0.02),
        "sg_ln_g": 1.0 + nrm(ks[12], (DEPTH, SG_WIDTH), 0.02),
        "sg_ln_b": nrm(ks[13], (DEPTH, SG_WIDTH), 0.01),
        "w_sp": nrm(ks[14], (DEPTH, SG_GROUPS, CHUNK, CHUNK), CHUNK ** -0.5),
        "b_sp": 1.0 + nrm(ks[15], (DEPTH, SG_GROUPS, CHUNK), 0.01),
        "w_pa": nrm(ks[16], (DEPTH, ATTN_W, D_MODEL), ATTN_W ** -0.5 * BETA),
        "w_pb": nrm(ks[17], (DEPTH, SG_WIDTH, D_MODEL), SG_WIDTH ** -0.5 * BETA),
        "w_o": nrm(ks[18], (DEPTH, D_MODEL, D_MODEL), D_MODEL ** -0.5 * BETA),
        "ln1_g": 1.0 + nrm(ks[19], (DEPTH, D_MODEL), 0.02),
        "ln1_b": nrm(ks[20], (DEPTH, D_MODEL), 0.01),
        "w_rg": nrm(ks[21], (DEPTH, D_MODEL, N_EGROUPS), D_MODEL ** -0.5),
        "b_rg": nrm(ks[22], (DEPTH, N_EGROUPS), 0.01),
        "w_re": nrm(ks[23], (DEPTH, D_MODEL, N_EXPERTS), D_MODEL ** -0.5),
        "b_re": nrm(ks[24], (DEPTH, N_EXPERTS), 0.01),
        "w_gate": nrm(ks[25], (DEPTH, N_EXPERTS, D_MODEL, D_EXPERT), D_MODEL ** -0.5 * BETA),
        "w_up": nrm(ks[26], (DEPTH, N_EXPERTS, D_MODEL, D_EXPERT), D_MODEL ** -0.5 * BETA),
        "w_down": nrm(ks[27], (DEPTH, N_EXPERTS, D_EXPERT, D_MODEL), D_EXPERT ** -0.5 * BETA),
        "ln2_g": 1.0 + nrm(ks[28], (DEPTH, D_MODEL), 0.02),
        "ln2_b": nrm(ks[29], (DEPTH, D_MODEL), 0.01),
    }


def reference(x_prompt, x_sample, cache_k, cache_v, page_table, w_in, b_in, lam_q1, lam_k1, lam_q2, lam_k2,
              subln_g, sg_ln_g, sg_ln_b, w_sp, b_sp, w_pa, w_pb, w_o, ln1_g, ln1_b, w_rg, b_rg, w_re, b_re,
              w_gate, w_up, w_down, ln2_g, ln2_b):
    yp, ys = x_prompt, x_sample
    kp_rows, vp_rows, ks_rows, vs_rows, sgv_rows = [], [], [], [], []
    for l in range(DEPTH):
        lam_i = lambda_init(l)
        lam = (jnp.exp(jnp.sum(lam_q1[l].astype(jnp.float32) * lam_k1[l].astype(jnp.float32)))
               - jnp.exp(jnp.sum(lam_q2[l].astype(jnp.float32) * lam_k2[l].astype(jnp.float32))) + lam_i)

        def block(x, attend):
            q, k, v, u, vsg, ga, gb = mixer_inputs(x, w_in[l], b_in[l], sg_ln_g[l], sg_ln_b[l])
            ao = subln(attend(q, k, v), subln_g[l], lam_i)
            so = spatial_gating(u, vsg, w_sp[l], b_sp[l])
            merged = ga * (ao @ w_pa[l]) + gb * (so @ w_pb[l])
            x = layer_norm(ALPHA * x + merged @ w_o[l], ln1_g[l], ln1_b[l])
            ff = hier_moe(x, w_rg[l], b_rg[l], w_re[l], b_re[l], w_gate[l], w_up[l], w_down[l])
            x = layer_norm(ALPHA * x + ff, ln2_g[l], ln2_b[l])
            return x, k, v, vsg

        yp, kp, vp, _ = block(yp, lambda q, k, v: prompt_attention(q, k, v, lam))
        ys, kn, vn, sgv = block(ys, lambda q, k, v: sample_attention(q, k, v, cache_k[l], cache_v[l], page_table, lam))
        kp_rows.append(kp)
        vp_rows.append(vp)
        ks_rows.append(kn)
        vs_rows.append(vn)
        sgv_rows.append(sgv)
    new_k_prompt = jnp.stack(kp_rows)
    new_v_prompt = jnp.stack(vp_rows)
    new_k_sample = jnp.stack(ks_rows)
    new_v_sample = jnp.stack(vs_rows)
    new_sgv_sample = jnp.stack(sgv_rows)
    return (yp, ys, new_k_prompt, new_v_prompt, new_k_sample, new_v_sample, new_sgv_sample)
```

```python
import functools
import math

import jax
import jax.numpy as jnp
from jax import lax
from jax.experimental import pallas as pl
from jax.experimental.pallas import tpu as pltpu

D_MODEL = 2048
N_HEADS = 8
HEAD_DIM = 64
KEY_DIM = 2 * HEAD_DIM
VAL_DIM = 2 * HEAD_DIM
Q_W = N_HEADS * KEY_DIM
SG_WIDTH = 1024
SG_GROUPS = 8
CHUNK = 128
N_EGROUPS = 4
EXP_PER_GROUP = 8
N_EXPERTS = N_EGROUPS * EXP_PER_GROUP
TOP_K = 2
D_EXPERT = 512
LN_EPS = 1e-5
NEG = -1e30
HEAD_SHIFT = N_HEADS.bit_length() - 1
assert 1 << HEAD_SHIFT == N_HEADS

LANES = 128
COL_BLOCK = 1024
ROW_TILE = 256
PROJ_ROW_TILE = 768
ATT_BLOCK = 512
PAGES_PER_STEP = 4
NEW_TOKEN_ROWS = 16
EXPERT_TILE = 256
VMEM_LIMIT = 56 * 1024 * 1024


def _lambda_init(layer_idx):
    return 0.8 - 0.6 * math.exp(-0.3 * layer_idx)


def _layer_norm_rows(x, g, b):
    mu = jnp.mean(x, axis=-1, keepdims=True)
    xc = x - mu
    var = jnp.mean(xc * xc, axis=-1, keepdims=True)
    return xc * lax.rsqrt(var + LN_EPS) * g + b


def _lam_value(lamp_ref, lam_i):
    a = jnp.sum(lamp_ref[0:1, :] * lamp_ref[1:2, :], axis=-1, keepdims=True)
    b = jnp.sum(lamp_ref[2:3, :] * lamp_ref[3:4, :], axis=-1, keepdims=True)
    return jnp.exp(a) - jnp.exp(b) + lam_i


def _proj_qkv_kernel(x_ref, w_ref, b_ref, o_ref):
    z = jnp.dot(x_ref[...], w_ref[...], preferred_element_type=jnp.float32) + b_ref[...]
    scale = jnp.where(pl.program_id(0) == 0, HEAD_DIM ** -0.5, 1.0).astype(jnp.float32)
    o_ref[...] = z * scale


def _proj_sg_kernel(x_ref, w_ref, b_ref, g_ref, bb_ref, o_ref):
    z = jnp.dot(x_ref[...], w_ref[...], preferred_element_type=jnp.float32) + b_ref[...]
    y = jax.nn.gelu(z)
    j = pl.program_id(0)

    @pl.when(j == 0)
    def _():
        o_ref[...] = y

    @pl.when(j == 1)
    def _():
        o_ref[...] = _layer_norm_rows(y, g_ref[...], bb_ref[...])


def _proj_gate_kernel(x_ref, w_ref, b_ref, o_ref):
    z = jnp.dot(x_ref[...], w_ref[...], preferred_element_type=jnp.float32) + b_ref[...]
    o_ref[...] = (1.0 / (1.0 + jnp.exp(-z))).astype(o_ref.dtype)


def _proj(body, xb, w, b, col_off, n_col, out_dtype, extra=(), name=None):
    nt = xb.shape[0]
    tm = PROJ_ROW_TILE
    off = col_off // COL_BLOCK
    extra_specs = [pl.BlockSpec((1, COL_BLOCK), lambda j, i: (0, 0)) for _ in extra]
    return pl.pallas_call(
        body,
        grid=(n_col // COL_BLOCK, nt // tm),
        in_specs=[
            pl.BlockSpec((tm, D_MODEL), lambda j, i: (i, 0)),
            pl.BlockSpec((D_MODEL, COL_BLOCK), lambda j, i: (0, off + j)),
            pl.BlockSpec((1, COL_BLOCK), lambda j, i: (0, off + j)),
        ] + extra_specs,
        out_specs=pl.BlockSpec((tm, COL_BLOCK), lambda j, i: (i, j)),
        out_shape=jax.ShapeDtypeStruct((nt, n_col), out_dtype),
        compiler_params=pltpu.CompilerParams(
            dimension_semantics=("arbitrary", "arbitrary"), vmem_limit_bytes=VMEM_LIMIT),
        name=name,
    )(xb, w, b, *extra)


def _flash_kernel(q_ref, k_ref, v_ref, lamp_ref, g_ref, o_ref,
                  kb_ref, vt_ref, qs_ref, m_ref, l_ref, acc_ref, *, lam_i, n_blk):
    tq = ATT_BLOCK
    qi = pl.program_id(1)

    @pl.when(qi == 0)
    def _():
        for c in range(n_blk):
            rows = slice(c * tq, (c + 1) * tq)
            kb_ref[c] = k_ref[rows, :].astype(jnp.bfloat16)
            vt_ref[c] = v_ref[rows, :].T.astype(jnp.bfloat16)

    qt = q_ref[...].T
    row = lax.broadcasted_iota(jnp.int32, qt.shape, 0)
    qs_ref[:, :tq] = jnp.where(row < HEAD_DIM, qt, 0.0).astype(jnp.bfloat16)
    qs_ref[:, tq:] = jnp.where(row >= HEAD_DIM, qt, 0.0).astype(jnp.bfloat16)
    m_ref[...] = jnp.full(m_ref.shape, NEG, jnp.float32)
    l_ref[...] = jnp.zeros(l_ref.shape, jnp.float32)
    acc_ref[...] = jnp.zeros(acc_ref.shape, jnp.float32)

    def step(ki, masked):
        s = jnp.dot(kb_ref[ki], qs_ref[...], preferred_element_type=jnp.float32)
        if masked:
            kpos = lax.broadcasted_iota(jnp.int32, s.shape, 0)
            qpos = lax.broadcasted_iota(jnp.int32, s.shape, 1)
            qpos = jnp.where(qpos >= tq, qpos - tq, qpos)
            s = jnp.where(kpos <= qpos, s, NEG)
        m_prev = m_ref[...]
        m_cur = jnp.maximum(m_prev, jnp.max(s, axis=0, keepdims=True))
        alpha = jnp.exp(m_prev - m_cur)
        p = jnp.exp(s - m_cur)
        l_ref[...] = alpha * l_ref[...] + jnp.sum(p, axis=0, keepdims=True)
        acc_ref[...] = alpha * acc_ref[...] + jnp.dot(
            vt_ref[ki], p.astype(jnp.bfloat16), preferred_element_type=jnp.float32)
        m_ref[...] = m_cur

    def body(ki, carry):
        step(ki, False)
        return carry

    lax.fori_loop(0, qi, body, 0)
    step(qi, True)

    lam = _lam_value(lamp_ref, lam_i)
    o = (acc_ref[:, :tq] / l_ref[:, :tq]) - lam * (acc_ref[:, tq:] / l_ref[:, tq:])
    ms = jnp.mean(o * o, axis=0, keepdims=True)
    r = o * lax.rsqrt(ms + LN_EPS) * g_ref[...] * (1.0 - lam_i)
    o_ref[...] = r.T.astype(o_ref.dtype)


def _flash(qkv, lamp, g_col, lam_i, batch, seq):
    tq = ATT_BLOCK
    n_blk = seq // tq
    kcol = Q_W // KEY_DIM
    vcol = 2 * Q_W // KEY_DIM
    return pl.pallas_call(
        functools.partial(_flash_kernel, lam_i=lam_i, n_blk=n_blk),
        grid=(batch * N_HEADS, n_blk),
        in_specs=[
            pl.BlockSpec((tq, KEY_DIM), lambda bh, qi: ((bh // N_HEADS) * n_blk + qi, bh % N_HEADS)),
            pl.BlockSpec((seq, KEY_DIM), lambda bh, qi: (bh // N_HEADS, kcol + bh % N_HEADS)),
            pl.BlockSpec((seq, VAL_DIM), lambda bh, qi: (bh // N_HEADS, vcol + bh % N_HEADS)),
            pl.BlockSpec((4, HEAD_DIM), lambda bh, qi: (0, 0)),
            pl.BlockSpec((VAL_DIM, 1), lambda bh, qi: (0, 0)),
        ],
        out_specs=pl.BlockSpec((tq, VAL_DIM), lambda bh, qi: ((bh // N_HEADS) * n_blk + qi, bh % N_HEADS)),
        out_shape=jax.ShapeDtypeStruct((batch * seq, N_HEADS * VAL_DIM), jnp.bfloat16),
        scratch_shapes=[
            pltpu.VMEM((n_blk, tq, KEY_DIM), jnp.bfloat16),
            pltpu.VMEM((n_blk, VAL_DIM, tq), jnp.bfloat16),
            pltpu.VMEM((KEY_DIM, 2 * tq), jnp.bfloat16),
            pltpu.VMEM((1, 2 * tq), jnp.float32),
            pltpu.VMEM((1, 2 * tq), jnp.float32),
            pltpu.VMEM((VAL_DIM, 2 * tq), jnp.float32),
        ],
        compiler_params=pltpu.CompilerParams(
            dimension_semantics=("arbitrary", "arbitrary"), vmem_limit_bytes=VMEM_LIMIT),
        name="flash_diff_attn",
    )(qkv, qkv, qkv, lamp, g_col)


def _decode_kernel(pt_ref, qbig_ref, *refs, lam_i, dec_seq):
    del pt_ref
    pp = PAGES_PER_STEP
    k_refs = refs[:pp]
    v_refs = refs[pp:2 * pp]
    kn_ref, vn_ref, lamp_ref, g_ref, o_ref, m_ref, l_ref, acc_ref = refs[2 * pp:]
    p_id = pl.program_id(1)
    nq = qbig_ref.shape[0]
    half = nq // 2

    @pl.when(p_id == 0)
    def _():
        m_ref[...] = jnp.full(m_ref.shape, NEG, jnp.float32)
        l_ref[...] = jnp.zeros(l_ref.shape, jnp.float32)
        acc_ref[...] = jnp.zeros(acc_ref.shape, jnp.float32)

    def update(k3, v3, causal):
        nk = k3.shape[0] * N_HEADS
        k2 = k3.reshape(nk, KEY_DIM).astype(jnp.bfloat16)
        v2 = v3.reshape(nk, VAL_DIM).astype(jnp.bfloat16)
        s = lax.dot_general(qbig_ref[...], k2, (((1,), (1,)), ((), ())),
                            preferred_element_type=jnp.float32)
        qrow = lax.broadcasted_iota(jnp.int32, s.shape, 0)
        kcol = lax.broadcasted_iota(jnp.int32, s.shape, 1)
        ok = (qrow & (N_HEADS - 1)) == (kcol & (N_HEADS - 1))
        if causal:
            q_t = (qrow >> HEAD_SHIFT) & (dec_seq - 1)
            ok = ok & ((kcol >> HEAD_SHIFT) <= q_t)
        s = jnp.where(ok, s, NEG)
        m_prev = m_ref[...]
        m_cur = jnp.maximum(m_prev, jnp.max(s, axis=-1, keepdims=True))
        alpha = jnp.exp(m_prev - m_cur)
        p = jnp.exp(s - m_cur)
        l_ref[...] = alpha * l_ref[...] + jnp.sum(p, axis=-1, keepdims=True)
        acc_ref[...] = alpha * acc_ref[...] + jnp.dot(
            p.astype(jnp.bfloat16), v2, preferred_element_type=jnp.float32)
        m_ref[...] = m_cur

    for r in range(pp):
        update(k_refs[r][...], v_refs[r][...], False)

    @pl.when(p_id == pl.num_programs(1) - 1)
    def _():
        update(kn_ref[...], vn_ref[...], True)
        lam = _lam_value(lamp_ref, lam_i)
        o1 = acc_ref[0:half, :] / l_ref[0:half, :]
        o2 = acc_ref[half:nq, :] / l_ref[half:nq, :]
        d = o1 - lam * o2
        ms = jnp.mean(d * d, axis=-1, keepdims=True)
        o_ref[...] = d * lax.rsqrt(ms + LN_EPS) * g_ref[...] * (1.0 - lam_i)


def _decode(page_table, qbig, cache_k, cache_v, layer, kn, vn, lamp, g_row, lam_i, dec_seq):
    dec_batch, n_pages = page_table.shape
    pp = PAGES_PER_STEP
    page = cache_k.shape[2]
    nq = qbig.shape[1]
    new_rows = kn.shape[1]

    def page_spec(r):
        return pl.BlockSpec((None, None, page, N_HEADS, KEY_DIM),
                            lambda b, p, pt: (layer, pt[b, p * pp + r], 0, 0, 0))

    grid_spec = pltpu.PrefetchScalarGridSpec(
        num_scalar_prefetch=1,
        grid=(dec_batch, n_pages // pp),
        in_specs=[pl.BlockSpec((None, nq, KEY_DIM), lambda b, p, pt: (b, 0, 0))]
        + [page_spec(r) for r in range(pp)] + [page_spec(r) for r in range(pp)]
        + [pl.BlockSpec((None, new_rows, N_HEADS, KEY_DIM), lambda b, p, pt: (b, 0, 0, 0)),
           pl.BlockSpec((None, new_rows, N_HEADS, VAL_DIM), lambda b, p, pt: (b, 0, 0, 0)),
           pl.BlockSpec((4, HEAD_DIM), lambda b, p, pt: (0, 0)),
           pl.BlockSpec((1, VAL_DIM), lambda b, p, pt: (0, 0))],
        out_specs=pl.BlockSpec((None, nq // 2, VAL_DIM), lambda b, p, pt: (b, 0, 0)),
        scratch_shapes=[
            pltpu.VMEM((nq, 1), jnp.float32),
            pltpu.VMEM((nq, 1), jnp.float32),
            pltpu.VMEM((nq, VAL_DIM), jnp.float32),
        ],
    )
    return pl.pallas_call(
        functools.partial(_decode_kernel, lam_i=lam_i, dec_seq=dec_seq),
        grid_spec=grid_spec,
        out_shape=jax.ShapeDtypeStruct((dec_batch, nq // 2, VAL_DIM), jnp.float32),
        compiler_params=pltpu.CompilerParams(
            dimension_semantics=("arbitrary", "arbitrary"), vmem_limit_bytes=VMEM_LIMIT),
        name="paged_diff_attn",
    )(page_table, qbig, *([cache_k] * pp), *([cache_v] * pp), kn, vn, lamp, g_row)


def _merge_kernel(ao_ref, sg_ref, gate_ref, x_ref, wmix_ref, bmix_ref, wpa_ref, wpb_ref, wo_ref,
                  g1_ref, b1_ref, wr_ref, br_ref, x1_ref, route_ref, so_ref, *, alpha, n_prompt_tiles):
    tm = x_ref.shape[0]
    sel = (pl.program_id(0) >= n_prompt_tiles).astype(jnp.int32)
    for c in range(tm // CHUNK):
        rows = slice(c * CHUNK, (c + 1) * CHUNK)
        for g in range(SG_GROUPS):
            cols = slice(g * CHUNK, (g + 1) * CHUNK)
            vs = sg_ref[rows, SG_WIDTH + g * CHUNK:SG_WIDTH + (g + 1) * CHUNK].astype(jnp.bfloat16)
            mixed = jnp.dot(wmix_ref[sel, g], vs, preferred_element_type=jnp.float32) + bmix_ref[sel, :, cols]
            so_ref[rows, cols] = (sg_ref[rows, cols] * mixed).astype(jnp.bfloat16)

    a = jnp.dot(ao_ref[...], wpa_ref[...], preferred_element_type=jnp.float32)
    bm = jnp.dot(so_ref[...], wpb_ref[...], preferred_element_type=jnp.float32)
    merged = gate_ref[:, :D_MODEL].astype(jnp.float32) * a + gate_ref[:, D_MODEL:].astype(jnp.float32) * bm
    y = jnp.dot(merged.astype(jnp.bfloat16), wo_ref[...], preferred_element_type=jnp.float32)
    x1 = _layer_norm_rows(alpha * x_ref[...] + y, g1_ref[...], b1_ref[...])
    x1_ref[...] = x1

    logits = jnp.dot(x1.astype(jnp.bfloat16), wr_ref[...], preferred_element_type=jnp.float32) + br_ref[...]
    lane = lax.broadcasted_iota(jnp.int32, logits.shape, 1)
    ninf = -jnp.inf
    gmask = lane < N_EGROUPS
    gl = jnp.where(gmask, logits, ninf)
    gmax = jnp.max(gl, axis=-1, keepdims=True)
    gi = jnp.min(jnp.where(gl == gmax, lane, LANES), axis=-1, keepdims=True)
    gsum = jnp.sum(jnp.where(gmask, jnp.exp(gl - gmax), 0.0), axis=-1, keepdims=True)
    gw = 1.0 / gsum
    emask = (lane >= N_EGROUPS) & (lane < N_EGROUPS + N_EXPERTS) & (((lane - N_EGROUPS) // EXP_PER_GROUP) == gi)
    el = jnp.where(emask, logits, ninf)
    v1 = jnp.max(el, axis=-1, keepdims=True)
    i1 = jnp.min(jnp.where(el == v1, lane, LANES), axis=-1, keepdims=True)
    el2 = jnp.where(lane == i1, ninf, el)
    v2 = jnp.max(el2, axis=-1, keepdims=True)
    i2 = jnp.min(jnp.where(el2 == v2, lane, LANES), axis=-1, keepdims=True)
    e2 = jnp.exp(v2 - v1)
    den = 1.0 + e2
    tw1 = (1.0 / den) * gw
    tw2 = (e2 / den) * gw
    id1 = (i1 - N_EGROUPS).astype(jnp.float32)
    id2 = (i2 - N_EGROUPS).astype(jnp.float32)
    route_ref[...] = jnp.where(lane == 0, id1, jnp.where(lane == 1, id2,
                               jnp.where(lane == 2, tw1, jnp.where(lane == 3, tw2, 0.0))))


def _merge(ao, sg, gate, x, wmix, bmix, wpa, wpb, wo, g1, b1, wr, br, alpha, n_prompt_tiles):
    nt = x.shape[0]
    tm = ROW_TILE

    def full(arr):
        nd = arr.ndim
        return pl.BlockSpec(arr.shape, lambda i, _nd=nd: (0,) * _nd, pipeline_mode=pl.Buffered(1))

    return pl.pallas_call(
        functools.partial(_merge_kernel, alpha=alpha, n_prompt_tiles=n_prompt_tiles),
        grid=(nt // tm,),
        in_specs=[
            pl.BlockSpec((tm, ao.shape[1]), lambda i: (i, 0)),
            pl.BlockSpec((tm, sg.shape[1]), lambda i: (i, 0)),
            pl.BlockSpec((tm, gate.shape[1]), lambda i: (i, 0)),
            pl.BlockSpec((tm, D_MODEL), lambda i: (i, 0)),
            full(wmix), full(bmix), full(wpa), full(wpb), full(wo), full(g1), full(b1), full(wr), full(br),
        ],
        out_specs=[pl.BlockSpec((tm, D_MODEL), lambda i: (i, 0)),
                   pl.BlockSpec((tm, LANES), lambda i: (i, 0))],
        out_shape=[jax.ShapeDtypeStruct((nt, D_MODEL), jnp.float32),
                   jax.ShapeDtypeStruct((nt, LANES), jnp.float32)],
        scratch_shapes=[pltpu.VMEM((tm, SG_WIDTH), jnp.bfloat16)],
        compiler_params=pltpu.CompilerParams(
            dimension_semantics=("arbitrary",), vmem_limit_bytes=VMEM_LIMIT),
        name="merge_ln_router",
    )(ao, sg, gate, x, wmix, bmix, wpa, wpb, wo, g1, b1, wr, br)


def _moe_kernel(te_ref, na_ref, tok_ref, tok_next_ref, dst_ref, x_hbm, wg_ref, wu_ref, wd_ref, y_hbm,
                xbuf, ybuf, gsem, ssem):
    del te_ref
    te = EXPERT_TILE
    t = pl.program_id(0)
    n_active = na_ref[0]
    slot = t % 2

    def gather_row(r, idx_ref, dst_slot):
        return pltpu.make_async_copy(x_hbm.at[pl.ds(idx_ref[0, 0, r], 1)],
                                     xbuf.at[dst_slot, pl.ds(r, 1)], gsem.at[dst_slot])

    def scatter_row(r, src_slot):
        return pltpu.make_async_copy(ybuf.at[src_slot, pl.ds(r, 1)],
                                     y_hbm.at[pl.ds(dst_ref[0, 0, r], 1)], ssem.at[src_slot])

    def start_gather(idx_ref, dst_slot):
        def body(r, c):
            gather_row(r, idx_ref, dst_slot).start()
            return c
        lax.fori_loop(0, te, body, 0, unroll=8)

    def wait_gather(s):
        pltpu.make_async_copy(x_hbm.at[pl.ds(0, te)], xbuf.at[s], gsem.at[s]).wait()

    def wait_scatter(s):
        pltpu.make_async_copy(ybuf.at[s], y_hbm.at[pl.ds(0, te)], ssem.at[s]).wait()

    @pl.when(t == 0)
    def _():
        ybuf[1] = jnp.zeros(ybuf.shape[1:], jnp.float32)
        spare = pltpu.make_async_copy(ybuf.at[1], y_hbm.at[pl.ds(y_hbm.shape[0] - te, te)], ssem.at[1])
        spare.start()
        spare.wait()

    @pl.when((t == 0) & (n_active > 0))
    def _():
        start_gather(tok_ref, 0)

    @pl.when(t < n_active)
    def _():
        wait_gather(slot)

        @pl.when(t + 1 < n_active)
        def _():
            start_gather(tok_next_ref, 1 - slot)

        xb = xbuf[slot].astype(jnp.bfloat16)
        hg = jnp.dot(xb, wg_ref[...], preferred_element_type=jnp.float32)
        hu = jnp.dot(xb, wu_ref[...], preferred_element_type=jnp.float32)
        act = (hg * (1.0 / (1.0 + jnp.exp(-hg))) * hu).astype(jnp.bfloat16)
        y = jnp.dot(act, wd_ref[...], preferred_element_type=jnp.float32)

        @pl.when(t >= 2)
        def _():
            wait_scatter(slot)

        ybuf[slot] = y

        def body(r, c):
            scatter_row(r, slot).start()
            return c
        lax.fori_loop(0, te, body, 0, unroll=8)

    @pl.when(t == pl.num_programs(0) - 1)
    def _():
        @pl.when(n_active >= 1)
        def _():
            wait_scatter((n_active - 1) % 2)

        @pl.when(n_active >= 2)
        def _():
            wait_scatter(n_active % 2)


def _moe(tile_expert, n_active, tok3, dst3, x1, wg, wu, wd, n_out_rows):
    n_tiles = tok3.shape[0]
    te = EXPERT_TILE
    grid_spec = pltpu.PrefetchScalarGridSpec(
        num_scalar_prefetch=2,
        grid=(n_tiles,),
        in_specs=[
            pl.BlockSpec((1, 1, te), lambda t, e, n: (t, 0, 0), memory_space=pltpu.SMEM),
            pl.BlockSpec((1, 1, te), lambda t, e, n: (jnp.minimum(t + 1, n_tiles - 1), 0, 0),
                         memory_space=pltpu.SMEM),
            pl.BlockSpec((1, 1, te), lambda t, e, n: (t, 0, 0), memory_space=pltpu.SMEM),
            pl.BlockSpec(memory_space=pl.ANY),
            pl.BlockSpec((None, D_MODEL, D_EXPERT), lambda t, e, n: (e[t], 0, 0)),
            pl.BlockSpec((None, D_MODEL, D_EXPERT), lambda t, e, n: (e[t], 0, 0)),
            pl.BlockSpec((None, D_EXPERT, D_MODEL), lambda t, e, n: (e[t], 0, 0)),
        ],
        out_specs=pl.BlockSpec(memory_space=pl.ANY),
        scratch_shapes=[
            pltpu.VMEM((2, te, D_MODEL), jnp.float32),
            pltpu.VMEM((2, te, D_MODEL), jnp.float32),
            pltpu.SemaphoreType.DMA((2,)),
            pltpu.SemaphoreType.DMA((2,)),
        ],
    )
    return pl.pallas_call(
        _moe_kernel,
        grid_spec=grid_spec,
        out_shape=jax.ShapeDtypeStruct((n_out_rows, D_MODEL), jnp.float32),
        compiler_params=pltpu.CompilerParams(
            dimension_semantics=("arbitrary",), vmem_limit_bytes=VMEM_LIMIT),
        name="moe_grouped",
    )(tile_expert, n_active, tok3, tok3, dst3, x1, wg, wu, wd)


def _moe_plan(route, nt):
    te = EXPERT_TILE
    na = nt * TOP_K
    n_tiles = na // te + N_EXPERTS
    n_rows = n_tiles * te
    ef = route[:, :TOP_K].astype(jnp.int32).reshape(na)
    onehot = (ef[:, None] == jnp.arange(N_EXPERTS, dtype=jnp.int32)[None, :]).astype(jnp.int32)
    csum = jnp.cumsum(onehot, axis=0)
    counts = csum[-1]
    rank = jnp.take_along_axis(csum, ef[:, None], axis=1)[:, 0] - 1
    pcounts = ((counts + te - 1) // te) * te
    pend = jnp.cumsum(pcounts)
    pstart = pend - pcounts
    pos = pstart[ef] + rank
    a = jnp.arange(na, dtype=jnp.int32)
    tok = jnp.zeros((n_rows,), jnp.int32).at[pos].set(a // TOP_K, unique_indices=True)
    dst = (na + jnp.arange(n_rows, dtype=jnp.int32) % te).at[pos].set(a, unique_indices=True)
    n_active = (pend[-1] // te).astype(jnp.int32).reshape(1)
    tile_start = jnp.minimum(jnp.arange(n_tiles, dtype=jnp.int32) * te, pend[-1] - 1)
    tile_expert = jnp.minimum(jnp.searchsorted(pend, tile_start, side="right"), N_EXPERTS - 1).astype(jnp.int32)
    return tile_expert, n_active, tok.reshape(n_tiles, 1, te), dst.reshape(n_tiles, 1, te)


def _combine_kernel(x1_ref, y_ref, route_ref, g_ref, b_ref, x2_ref, x2b_ref, *, alpha):
    w1 = route_ref[:, TOP_K:TOP_K + 1]
    w2 = route_ref[:, TOP_K + 1:TOP_K + 2]
    ff = w1 * y_ref[:, :D_MODEL] + w2 * y_ref[:, D_MODEL:]
    x2 = _layer_norm_rows(alpha * x1_ref[...] + ff, g_ref[...], b_ref[...])
    x2_ref[...] = x2
    x2b_ref[...] = x2.astype(jnp.bfloat16)


def _combine(x1, y2, route, g2, b2, alpha):
    nt = x1.shape[0]
    tm = ROW_TILE
    return pl.pallas_call(
        functools.partial(_combine_kernel, alpha=alpha),
        grid=(nt // tm,),
        in_specs=[
            pl.BlockSpec((tm, D_MODEL), lambda i: (i, 0)),
            pl.BlockSpec((tm, TOP_K * D_MODEL), lambda i: (i, 0)),
            pl.BlockSpec((tm, LANES), lambda i: (i, 0)),
            pl.BlockSpec((1, D_MODEL), lambda i: (0, 0)),
            pl.BlockSpec((1, D_MODEL), lambda i: (0, 0)),
        ],
        out_specs=[pl.BlockSpec((tm, D_MODEL), lambda i: (i, 0)),
                   pl.BlockSpec((tm, D_MODEL), lambda i: (i, 0))],
        out_shape=[jax.ShapeDtypeStruct((nt, D_MODEL), jnp.float32),
                   jax.ShapeDtypeStruct((nt, D_MODEL), jnp.bfloat16)],
        compiler_params=pltpu.CompilerParams(
            dimension_semantics=("arbitrary",), vmem_limit_bytes=VMEM_LIMIT),
        name="combine_ln",
    )(x1, y2, route, g2, b2)


def kernel(x_prompt, x_sample, cache_k, cache_v, page_table, w_in, b_in, lam_q1, lam_k1, lam_q2, lam_k2, subln_g, sg_ln_g, sg_ln_b, w_sp, b_sp, w_pa, w_pb, w_o, ln1_g, ln1_b, w_rg, b_rg, w_re, b_re, w_gate, w_up, w_down, ln2_g, ln2_b):
    batch, seq, _ = x_prompt.shape
    dec_batch, dec_seq, _ = x_sample.shape
    depth, n_pool, page = cache_k.shape[:3]
    n_prompt = batch * seq
    n_sample = dec_batch * dec_seq
    assert seq % ATT_BLOCK == 0 and n_prompt % ROW_TILE == 0
    assert n_sample <= CHUNK and page_table.shape[1] % PAGES_PER_STEP == 0
    assert dec_seq & (dec_seq - 1) == 0 and dec_seq <= NEW_TOKEN_ROWS
    nt = -(-(n_prompt + n_sample) // PROJ_ROW_TILE) * PROJ_ROW_TILE
    alpha = (2 * depth) ** 0.25
    bf16 = jnp.bfloat16
    f32 = jnp.float32

    x = jnp.concatenate([x_prompt.reshape(n_prompt, D_MODEL), x_sample.reshape(n_sample, D_MODEL),
                         jnp.zeros((nt - n_prompt - n_sample, D_MODEL), f32)], axis=0)
    xb = x.astype(bf16)

    ridx = jnp.arange(CHUNK)
    rt = ridx % dec_seq
    rs = ridx // dec_seq
    rvalid = ridx < n_sample
    smask = ((rt[None, :] <= rt[:, None]) & (rs[None, :] == rs[:, None])
             & rvalid[None, :] & rvalid[:, None]).astype(f32)
    tril = jnp.tril(jnp.ones((CHUNK, CHUNK), f32))
    eye_m = jnp.eye(2, dtype=f32)

    kp_rows, vp_rows, ks_rows, vs_rows, sgv_rows = [], [], [], [], []
    for l in range(depth):
        lam_i = _lambda_init(l)
        w_in_b = w_in[l].astype(bf16)
        b_row = b_in[l].reshape(1, -1)
        qkv = _proj(_proj_qkv_kernel, xb, w_in_b, b_row, 0, 3 * Q_W, f32, name="proj_qkv")
        sg = _proj(_proj_sg_kernel, xb, w_in_b, b_row, 3 * Q_W, 2 * SG_WIDTH, f32,
                   extra=(sg_ln_g[l].reshape(1, -1), sg_ln_b[l].reshape(1, -1)), name="proj_sg")
        gate = _proj(_proj_gate_kernel, xb, w_in_b, b_row, 3 * Q_W + 2 * SG_WIDTH, 2 * D_MODEL, bf16,
                     name="proj_gate")

        lamp = jnp.stack([lam_q1[l], lam_k1[l], lam_q2[l], lam_k2[l]]).astype(f32)
        ao_p = _flash(qkv, lamp, subln_g[l].reshape(VAL_DIM, 1), lam_i, batch, seq)

        s_rows = slice(n_prompt, n_prompt + n_sample)
        q5 = qkv[s_rows, :Q_W].reshape(dec_batch, dec_seq, N_HEADS, 2, HEAD_DIM)
        qbig = jnp.einsum("bthmd,mn->bmthnd", q5, eye_m).reshape(
            dec_batch, 2 * dec_seq * N_HEADS, KEY_DIM).astype(bf16)
        k_new = qkv[s_rows, Q_W:2 * Q_W].reshape(dec_batch, dec_seq, N_HEADS, KEY_DIM)
        v_new = qkv[s_rows, 2 * Q_W:].reshape(dec_batch, dec_seq, N_HEADS, VAL_DIM)
        pad_new = ((0, 0), (0, NEW_TOKEN_ROWS - dec_seq), (0, 0), (0, 0))
        ao_s = _decode(page_table, qbig, cache_k, cache_v, l, jnp.pad(k_new, pad_new), jnp.pad(v_new, pad_new),
                       lamp, subln_g[l].reshape(1, -1), lam_i, dec_seq)
        ao = jnp.concatenate([ao_p, ao_s.reshape(n_sample, -1).astype(bf16),
                              jnp.zeros((nt - n_prompt - n_sample, ao_p.shape[1]), bf16)], axis=0)

        wm_p = w_sp[l] * tril
        wm_s = w_sp[l][:, rt[:, None], rt[None, :]] * smask
        wmix = jnp.stack([wm_p, wm_s]).astype(bf16)
        bm_p = jnp.repeat(b_sp[l].T, CHUNK, axis=1)
        bmix = jnp.stack([bm_p, bm_p[rt]])
        wr = jnp.concatenate([w_rg[l], w_re[l],
                              jnp.zeros((D_MODEL, LANES - N_EGROUPS - N_EXPERTS), f32)], axis=1).astype(bf16)
        br = jnp.concatenate([b_rg[l], b_re[l], jnp.zeros((LANES - N_EGROUPS - N_EXPERTS,), f32)]).reshape(1, -1)
        x1, route = _merge(ao, sg, gate, x, wmix, bmix, w_pa[l].astype(bf16), w_pb[l].astype(bf16),
                           w_o[l].astype(bf16), ln1_g[l].reshape(1, -1), ln1_b[l].reshape(1, -1), wr, br,
                           alpha, n_prompt // ROW_TILE)

        tile_expert, n_active, tok3, dst3 = _moe_plan(route, nt)
        y2 = _moe(tile_expert, n_active, tok3, dst3, x1, w_gate[l].astype(bf16), w_up[l].astype(bf16),
                  w_down[l].astype(bf16), nt * TOP_K + EXPERT_TILE)
        y2 = y2.reshape(-1, TOP_K * D_MODEL)
        x, xb = _combine(x1, y2, route, ln2_g[l].reshape(1, -1), ln2_b[l].reshape(1, -1), alpha)

        kp_rows.append(qkv[:n_prompt, Q_W:2 * Q_W].reshape(batch, seq, N_HEADS, KEY_DIM))
        vp_rows.append(qkv[:n_prompt, 2 * Q_W:].reshape(batch, seq, N_HEADS, VAL_DIM))
        ks_rows.append(k_new)
        vs_rows.append(v_new)
        sgv_rows.append(sg[s_rows, SG_WIDTH:].reshape(dec_batch, dec_seq, SG_WIDTH))

    yp = x[:n_prompt].reshape(batch, seq, D_MODEL)
    ys = x[n_prompt:n_prompt + n_sample].reshape(dec_batch, dec_seq, D_MODEL)
    return (yp, ys, jnp.stack(kp_rows), jnp.stack(vp_rows), jnp.stack(ks_rows), jnp.stack(vs_rows),
            jnp.stack(sgv_rows))
```

```python
import functools
import math

import jax
import jax.numpy as jnp
from jax import lax
from jax.experimental import pallas as pl
from jax.experimental.pallas import tpu as pltpu

D_MODEL = 2048
N_HEADS = 8
HEAD_DIM = 64
KEY_DIM = 2 * HEAD_DIM
VAL_DIM = 2 * HEAD_DIM
Q_W = N_HEADS * KEY_DIM
SG_WIDTH = 1024
SG_GROUPS = 8
SG_CH = SG_WIDTH // SG_GROUPS
CHUNK = 128
N_EGROUPS = 4
EXP_PER_GROUP = 8
N_EXPERTS = N_EGROUPS * EXP_PER_GROUP
TOP_K = 2
D_EXPERT = 512
LN_EPS = 1e-5
NEG = -1e30
HEAD_SHIFT = N_HEADS.bit_length() - 1
assert 1 << HEAD_SHIFT == N_HEADS
HALF_D = D_MODEL // 2

LANES = 128
COL_BLOCK = 1024
ROW_TILE = 256
PROJ_ROW_TILE = 1024
ATT_BLOCK = 512
ATT_COLS = 512
PAGES_PER_STEP = 8
NEW_TOKEN_ROWS = 16
EXPERT_TILE = 256
VMEM_LIMIT = 56 * 1024 * 1024


def _lambda_init(layer_idx):
    return 0.8 - 0.6 * math.exp(-0.3 * layer_idx)


def _layer_norm_rows(x, g, b):
    mu = jnp.mean(x, axis=-1, keepdims=True)
    xc = x - mu
    var = jnp.mean(xc * xc, axis=-1, keepdims=True)
    return xc * lax.rsqrt(var + LN_EPS) * g + b


def _lam_value(lamp_ref, lam_i):
    a = jnp.sum(lamp_ref[0:1, :] * lamp_ref[1:2, :], axis=-1, keepdims=True)
    b = jnp.sum(lamp_ref[2:3, :] * lamp_ref[3:4, :], axis=-1, keepdims=True)
    return jnp.exp(a) - jnp.exp(b) + lam_i


def _params(n_axes):
    return pltpu.CompilerParams(dimension_semantics=("arbitrary",) * n_axes, vmem_limit_bytes=VMEM_LIMIT)


def _qkv_block(x_ref, w_ref, b_ref):
    z = jnp.dot(x_ref[...], w_ref[...], preferred_element_type=jnp.float32) + b_ref[...]
    scale = jnp.where(pl.program_id(0) == 0, HEAD_DIM ** -0.5, 1.0).astype(jnp.float32)
    return z * scale


def _proj_qkv_kernel(x_ref, w_ref, b_ref, o_ref):
    o_ref[...] = _qkv_block(x_ref, w_ref, b_ref)


def _proj_qkv_rows_kernel(x_ref, w_ref, b_ref, kin_ref, vin_ref, o_ref, ko_ref, vo_ref):
    del kin_ref, vin_ref
    z = _qkv_block(x_ref, w_ref, b_ref)
    o_ref[...] = z
    j = pl.program_id(0)

    def per_head(dst_ref):
        for h in range(N_HEADS):
            dst_ref[:, h, :] = z[:, h * KEY_DIM:(h + 1) * KEY_DIM]

    @pl.when(j == 1)
    def _():
        per_head(ko_ref)

    @pl.when(j == 2)
    def _():
        per_head(vo_ref)


def _proj_sg_kernel(x_ref, w_ref, b_ref, g_ref, bb_ref, o_ref):
    z = jnp.dot(x_ref[...], w_ref[...], preferred_element_type=jnp.float32) + b_ref[...]
    y = jax.nn.gelu(z)
    j = pl.program_id(0)

    @pl.when(j == 0)
    def _():
        o_ref[...] = y

    @pl.when(j == 1)
    def _():
        o_ref[...] = _layer_norm_rows(y, g_ref[...], bb_ref[...])


def _proj_gate_kernel(x_ref, w_ref, b_ref, o_ref):
    z = jnp.dot(x_ref[...], w_ref[...], preferred_element_type=jnp.float32) + b_ref[...]
    o_ref[...] = (1.0 / (1.0 + jnp.exp(-z))).astype(o_ref.dtype)


def _proj_in_specs(tm, off):
    return [
        pl.BlockSpec((tm, D_MODEL), lambda j, i: (i, 0)),
        pl.BlockSpec((D_MODEL, COL_BLOCK), lambda j, i: (0, off + j)),
        pl.BlockSpec((1, COL_BLOCK), lambda j, i: (0, off + j)),
    ]


def _proj(body, xb, w, b, col_off, n_col, out_dtype, extra=(), name=None):
    rows = xb.shape[0]
    tm = min(PROJ_ROW_TILE, rows)
    extra_specs = [pl.BlockSpec((1, COL_BLOCK), lambda j, i: (0, 0)) for _ in extra]
    return pl.pallas_call(
        body,
        grid=(n_col // COL_BLOCK, rows // tm),
        in_specs=_proj_in_specs(tm, col_off // COL_BLOCK) + extra_specs,
        out_specs=pl.BlockSpec((tm, COL_BLOCK), lambda j, i: (i, j)),
        out_shape=jax.ShapeDtypeStruct((rows, n_col), out_dtype),
        compiler_params=_params(2),
        name=name,
    )(xb, w, b, *extra)


def _proj_qkv_rows(xb, w, b, k_stack, v_stack, layer):
    rows = xb.shape[0]
    tm = PROJ_ROW_TILE
    n_i = rows // tm

    def k_rows(j, i):
        return (layer, jnp.where(j < 1, 0, jnp.where(j == 1, i, n_i - 1)), 0, 0)

    def v_rows(j, i):
        return (layer, jnp.where(j < 2, 0, i), 0, 0)

    any_spec = pl.BlockSpec(memory_space=pl.ANY)
    return pl.pallas_call(
        _proj_qkv_rows_kernel,
        grid=(3, n_i),
        in_specs=_proj_in_specs(tm, 0) + [any_spec, any_spec],
        out_specs=[pl.BlockSpec((tm, COL_BLOCK), lambda j, i: (i, j)),
                   pl.BlockSpec((None, tm, N_HEADS, KEY_DIM), k_rows),
                   pl.BlockSpec((None, tm, N_HEADS, VAL_DIM), v_rows)],
        out_shape=[jax.ShapeDtypeStruct((rows, 3 * Q_W), jnp.float32),
                   jax.ShapeDtypeStruct(k_stack.shape, k_stack.dtype),
                   jax.ShapeDtypeStruct(v_stack.shape, v_stack.dtype)],
        input_output_aliases={3: 1, 4: 2},
        compiler_params=_params(2),
        name="proj_qkv_rows",
    )(xb, w, b, k_stack, v_stack)


def _flash_kernel(q_ref, k_ref, v_ref, lamp_ref, g_ref, o_ref,
                  kb_ref, vt_ref, qs_ref, m_ref, l_ref, acc_ref, *, lam_i, n_blk):
    tq = ATT_BLOCK
    qi = pl.program_id(1)

    @pl.when(qi == 0)
    def _():
        for c in range(n_blk):
            rows = slice(c * tq, (c + 1) * tq)
            kb_ref[c] = k_ref[rows, :].astype(jnp.bfloat16)
            vt_ref[c] = v_ref[rows, :].T.astype(jnp.bfloat16)

    qt = q_ref[...].T
    row = lax.broadcasted_iota(jnp.int32, qt.shape, 0)
    qs_ref[:, :tq] = jnp.where(row < HEAD_DIM, qt, 0.0).astype(jnp.bfloat16)
    qs_ref[:, tq:] = jnp.where(row >= HEAD_DIM, qt, 0.0).astype(jnp.bfloat16)
    m_ref[...] = jnp.full(m_ref.shape, NEG, jnp.float32)
    l_ref[...] = jnp.zeros(l_ref.shape, jnp.float32)
    acc_ref[...] = jnp.zeros(acc_ref.shape, jnp.float32)

    def step(ki, masked):
        kblk = kb_ref[ki]
        vblk = vt_ref[ki]
        n_chains = 2 * tq // ATT_COLS

        def scores(c):
            cols = slice(c * ATT_COLS, (c + 1) * ATT_COLS)
            s = jnp.dot(kblk, qs_ref[:, cols], preferred_element_type=jnp.float32)
            if masked:
                kpos = lax.broadcasted_iota(jnp.int32, s.shape, 0)
                qpos = lax.broadcasted_iota(jnp.int32, s.shape, 1) + (c * ATT_COLS) % tq
                s = jnp.where(kpos <= qpos, s, NEG)
            return s

        new_state = []
        s_next = scores(0)
        for c in range(n_chains):
            cols = slice(c * ATT_COLS, (c + 1) * ATT_COLS)
            s = s_next
            if c + 1 < n_chains:
                s_next = scores(c + 1)
            m_prev = m_ref[:, cols]
            m_cur = jnp.maximum(m_prev, jnp.max(s, axis=0, keepdims=True))
            alpha = jnp.exp(m_prev - m_cur)
            p = jnp.exp(s - m_cur)
            l_new = alpha * l_ref[:, cols] + jnp.sum(p, axis=0, keepdims=True)
            acc_new = alpha * acc_ref[:, cols] + jnp.dot(
                vblk, p.astype(jnp.bfloat16), preferred_element_type=jnp.float32)
            new_state.append((cols, m_cur, l_new, acc_new))
        for cols, m_cur, l_new, acc_new in new_state:
            m_ref[:, cols] = m_cur
            l_ref[:, cols] = l_new
            acc_ref[:, cols] = acc_new

    def body(ki, carry):
        step(ki, False)
        return carry

    lax.fori_loop(0, qi, body, 0)
    step(qi, True)

    lam = _lam_value(lamp_ref, lam_i)
    o = (acc_ref[:, :tq] / l_ref[:, :tq]) - lam * (acc_ref[:, tq:] / l_ref[:, tq:])
    ms = jnp.mean(o * o, axis=0, keepdims=True)
    r = o * lax.rsqrt(ms + LN_EPS) * g_ref[...] * (1.0 - lam_i)
    o_ref[...] = r.T.astype(o_ref.dtype)


def _flash(qkv, lamp, g_col, lam_i, batch, seq):
    tq = ATT_BLOCK
    n_blk = seq // tq
    kcol = Q_W // KEY_DIM
    vcol = 2 * Q_W // KEY_DIM
    return pl.pallas_call(
        functools.partial(_flash_kernel, lam_i=lam_i, n_blk=n_blk),
        grid=(batch * N_HEADS, n_blk),
        in_specs=[
            pl.BlockSpec((tq, KEY_DIM), lambda bh, qi: ((bh // N_HEADS) * n_blk + qi, bh % N_HEADS)),
            pl.BlockSpec((seq, KEY_DIM), lambda bh, qi: (bh // N_HEADS, kcol + bh % N_HEADS)),
            pl.BlockSpec((seq, VAL_DIM), lambda bh, qi: (bh // N_HEADS, vcol + bh % N_HEADS)),
            pl.BlockSpec((4, HEAD_DIM), lambda bh, qi: (0, 0)),
            pl.BlockSpec((VAL_DIM, 1), lambda bh, qi: (0, 0)),
        ],
        out_specs=pl.BlockSpec((tq, VAL_DIM), lambda bh, qi: ((bh // N_HEADS) * n_blk + qi, bh % N_HEADS)),
        out_shape=jax.ShapeDtypeStruct((batch * seq, N_HEADS * VAL_DIM), jnp.bfloat16),
        scratch_shapes=[
            pltpu.VMEM((n_blk, tq, KEY_DIM), jnp.bfloat16),
            pltpu.VMEM((n_blk, VAL_DIM, tq), jnp.bfloat16),
            pltpu.VMEM((KEY_DIM, 2 * tq), jnp.bfloat16),
            pltpu.VMEM((1, 2 * tq), jnp.float32),
            pltpu.VMEM((1, 2 * tq), jnp.float32),
            pltpu.VMEM((VAL_DIM, 2 * tq), jnp.float32),
        ],
        compiler_params=_params(2),
        name="flash_diff_attn",
    )(qkv, qkv, qkv, lamp, g_col)


def _decode_kernel(pt_ref, qbig_ref, *refs, lam_i, dec_seq):
    del pt_ref
    pp = PAGES_PER_STEP
    k_refs = refs[:pp]
    v_refs = refs[pp:2 * pp]
    kn_ref, vn_ref, lamp_ref, g_ref, o_ref, m_ref, l_ref, acc_ref = refs[2 * pp:]
    p_id = pl.program_id(1)
    nq = qbig_ref.shape[0]
    half = nq // 2

    @pl.when(p_id == 0)
    def _():
        m_ref[...] = jnp.full(m_ref.shape, NEG, jnp.float32)
        l_ref[...] = jnp.zeros(l_ref.shape, jnp.float32)
        acc_ref[...] = jnp.zeros(acc_ref.shape, jnp.float32)

    def flat(refs_, width):
        parts = [r[...].reshape(r.shape[0] * N_HEADS, width) for r in refs_]
        rows = parts[0] if len(parts) == 1 else jnp.concatenate(parts, axis=0)
        return rows.astype(jnp.bfloat16)

    def update(k2, v2, causal):
        s = lax.dot_general(qbig_ref[...], k2, (((1,), (1,)), ((), ())),
                            preferred_element_type=jnp.float32)
        qrow = lax.broadcasted_iota(jnp.int32, s.shape, 0)
        kcol = lax.broadcasted_iota(jnp.int32, s.shape, 1)
        ok = (qrow & (N_HEADS - 1)) == (kcol & (N_HEADS - 1))
        if causal:
            q_t = (qrow >> HEAD_SHIFT) & (dec_seq - 1)
            ok = ok & ((kcol >> HEAD_SHIFT) <= q_t)
        s = jnp.where(ok, s, NEG)
        m_prev = m_ref[...]
        m_cur = jnp.maximum(m_prev, jnp.max(s, axis=-1, keepdims=True))
        alpha = jnp.exp(m_prev - m_cur)
        p = jnp.exp(s - m_cur)
        l_ref[...] = alpha * l_ref[...] + jnp.sum(p, axis=-1, keepdims=True)
        acc_ref[...] = alpha * acc_ref[...] + jnp.dot(
            p.astype(jnp.bfloat16), v2, preferred_element_type=jnp.float32)
        m_ref[...] = m_cur

    update(flat(k_refs, KEY_DIM), flat(v_refs, VAL_DIM), False)

    @pl.when(p_id == pl.num_programs(1) - 1)
    def _():
        update(flat([kn_ref], KEY_DIM), flat([vn_ref], VAL_DIM), True)
        lam = _lam_value(lamp_ref, lam_i)
        o1 = acc_ref[0:half, :] / l_ref[0:half, :]
        o2 = acc_ref[half:nq, :] / l_ref[half:nq, :]
        d = o1 - lam * o2
        ms = jnp.mean(d * d, axis=-1, keepdims=True)
        o_ref[...] = d * lax.rsqrt(ms + LN_EPS) * g_ref[...] * (1.0 - lam_i)


def _decode(page_table, qbig, cache_k, cache_v, layer, kn, vn, lamp, g_row, lam_i, dec_seq):
    dec_batch, n_pages = page_table.shape
    pp = PAGES_PER_STEP
    page = cache_k.shape[2]
    nq = qbig.shape[1]
    new_rows = kn.shape[1]

    def page_spec(r):
        return pl.BlockSpec((None, None, page, N_HEADS, KEY_DIM),
                            lambda b, p, pt: (layer, pt[b, p * pp + r], 0, 0, 0))

    grid_spec = pltpu.PrefetchScalarGridSpec(
        num_scalar_prefetch=1,
        grid=(dec_batch, n_pages // pp),
        in_specs=[pl.BlockSpec((None, nq, KEY_DIM), lambda b, p, pt: (b, 0, 0))]
        + [page_spec(r) for r in range(pp)] + [page_spec(r) for r in range(pp)]
        + [pl.BlockSpec((None, new_rows, N_HEADS, KEY_DIM), lambda b, p, pt: (b, 0, 0, 0)),
           pl.BlockSpec((None, new_rows, N_HEADS, VAL_DIM), lambda b, p, pt: (b, 0, 0, 0)),
           pl.BlockSpec((4, HEAD_DIM), lambda b, p, pt: (0, 0)),
           pl.BlockSpec((1, VAL_DIM), lambda b, p, pt: (0, 0))],
        out_specs=pl.BlockSpec((None, nq // 2, VAL_DIM), lambda b, p, pt: (b, 0, 0)),
        scratch_shapes=[
            pltpu.VMEM((nq, 1), jnp.float32),
            pltpu.VMEM((nq, 1), jnp.float32),
            pltpu.VMEM((nq, VAL_DIM), jnp.float32),
        ],
    )
    return pl.pallas_call(
        functools.partial(_decode_kernel, lam_i=lam_i, dec_seq=dec_seq),
        grid_spec=grid_spec,
        out_shape=jax.ShapeDtypeStruct((dec_batch, nq // 2, VAL_DIM), jnp.float32),
        compiler_params=_params(2),
        name="paged_diff_attn",
    )(page_table, qbig, *([cache_k] * pp), *([cache_v] * pp), kn, vn, lamp, g_row)


def _merge_kernel(ao_ref, sg_ref, gate_ref, x_ref, wmix_ref, bmix_ref, wpa_ref, wpb_ref, wo_ref,
                  g1_ref, b1_ref, wr_ref, br_ref, x1_ref, xpk_ref, route_ref, so_ref, *, alpha):
    tm = x_ref.shape[0]
    ch = wmix_ref.shape[-1]
    for c in range(tm // ch):
        rows = slice(c * ch, (c + 1) * ch)
        for g in range(SG_GROUPS):
            cols = slice(g * SG_CH, (g + 1) * SG_CH)
            vs = sg_ref[rows, SG_WIDTH + g * SG_CH:SG_WIDTH + (g + 1) * SG_CH].astype(jnp.bfloat16)
            mixed = jnp.dot(wmix_ref[g], vs, preferred_element_type=jnp.float32) + bmix_ref[:, cols]
            so_ref[rows, cols] = (sg_ref[rows, cols] * mixed).astype(jnp.bfloat16)

    a = jnp.dot(ao_ref[...], wpa_ref[...], preferred_element_type=jnp.float32)
    bm = jnp.dot(so_ref[...], wpb_ref[...], preferred_element_type=jnp.float32)
    merged = gate_ref[:, :D_MODEL].astype(jnp.float32) * a + gate_ref[:, D_MODEL:].astype(jnp.float32) * bm
    y = jnp.dot(merged.astype(jnp.bfloat16), wo_ref[...], preferred_element_type=jnp.float32)
    x1 = _layer_norm_rows(alpha * x_ref[...] + y, g1_ref[...], b1_ref[...])
    x1_ref[...] = x1

    x1b = x1.astype(jnp.bfloat16)
    lo = lax.bitcast_convert_type(x1b[:, :HALF_D].astype(jnp.float32), jnp.uint32)
    hi = lax.bitcast_convert_type(x1b[:, HALF_D:].astype(jnp.float32), jnp.uint32)
    xpk_ref[...] = (lo >> 16) | (hi & jnp.uint32(0xFFFF0000))

    logits = jnp.dot(x1b, wr_ref[...], preferred_element_type=jnp.float32) + br_ref[...]
    lane = lax.broadcasted_iota(jnp.int32, logits.shape, 1)
    ninf = -jnp.inf
    gmask = lane < N_EGROUPS
    gl = jnp.where(gmask, logits, ninf)
    gmax = jnp.max(gl, axis=-1, keepdims=True)
    gi = jnp.min(jnp.where(gl == gmax, lane, LANES), axis=-1, keepdims=True)
    gsum = jnp.sum(jnp.where(gmask, jnp.exp(gl - gmax), 0.0), axis=-1, keepdims=True)
    gw = 1.0 / gsum
    emask = (lane >= N_EGROUPS) & (lane < N_EGROUPS + N_EXPERTS) & (((lane - N_EGROUPS) // EXP_PER_GROUP) == gi)
    el = jnp.where(emask, logits, ninf)
    v1 = jnp.max(el, axis=-1, keepdims=True)
    i1 = jnp.min(jnp.where(el == v1, lane, LANES), axis=-1, keepdims=True)
    el2 = jnp.where(lane == i1, ninf, el)
    v2 = jnp.max(el2, axis=-1, keepdims=True)
    i2 = jnp.min(jnp.where(el2 == v2, lane, LANES), axis=-1, keepdims=True)
    e2 = jnp.exp(v2 - v1)
    den = 1.0 + e2
    tw1 = (1.0 / den) * gw
    tw2 = (e2 / den) * gw
    id1 = (i1 - N_EGROUPS).astype(jnp.float32)
    id2 = (i2 - N_EGROUPS).astype(jnp.float32)
    route_ref[...] = jnp.where(lane == 0, id1, jnp.where(lane == 1, id2,
                               jnp.where(lane == 2, tw1, jnp.where(lane == 3, tw2, 0.0))))


def _merge(ao, sg, gate, x, wmix, bmix, wpa, wpb, wo, g1, b1, wr, br, alpha):
    rows = x.shape[0]
    tm = min(ROW_TILE, rows)

    def full(arr):
        nd = arr.ndim
        return pl.BlockSpec(arr.shape, lambda i, _nd=nd: (0,) * _nd, pipeline_mode=pl.Buffered(1))

    return pl.pallas_call(
        functools.partial(_merge_kernel, alpha=alpha),
        grid=(rows // tm,),
        in_specs=[
            pl.BlockSpec((tm, ao.shape[1]), lambda i: (i, 0)),
            pl.BlockSpec((tm, sg.shape[1]), lambda i: (i, 0)),
            pl.BlockSpec((tm, gate.shape[1]), lambda i: (i, 0)),
            pl.BlockSpec((tm, D_MODEL), lambda i: (i, 0)),
            full(wmix), full(bmix), full(wpa), full(wpb), full(wo), full(g1), full(b1), full(wr), full(br),
        ],
        out_specs=[pl.BlockSpec((tm, D_MODEL), lambda i: (i, 0)),
                   pl.BlockSpec((tm, HALF_D), lambda i: (i, 0)),
                   pl.BlockSpec((tm, LANES), lambda i: (i, 0))],
        out_shape=[jax.ShapeDtypeStruct((rows, D_MODEL), jnp.float32),
                   jax.ShapeDtypeStruct((rows, HALF_D), jnp.uint32),
                   jax.ShapeDtypeStruct((rows, LANES), jnp.float32)],
        scratch_shapes=[pltpu.VMEM((tm, SG_WIDTH), jnp.bfloat16)],
        compiler_params=_params(1),
        name="merge_ln_router",
    )(ao, sg, gate, x, wmix, bmix, wpa, wpb, wo, g1, b1, wr, br)


def _moe_plan(route_p, route_s, n_tiles):
    te = EXPERT_TILE
    ef = jnp.concatenate([route_p[:, :TOP_K], route_s[:, :TOP_K]], axis=0).astype(jnp.int32).reshape(-1)
    onehot = (ef[:, None] == jnp.arange(N_EXPERTS, dtype=jnp.int32)[None, :]).astype(jnp.int32)
    csum = jnp.cumsum(onehot, axis=0)
    counts = csum[-1]
    rank = jnp.sum(csum * onehot, axis=1) - 1
    pcounts = ((counts + te - 1) // te) * te
    pend = jnp.cumsum(pcounts)
    pstart = pend - pcounts
    pos = jnp.sum(onehot * pstart[None, :], axis=1) + rank
    n_active = (pend[-1] // te).astype(jnp.int32).reshape(1)
    tile_start = jnp.minimum(jnp.arange(n_tiles, dtype=jnp.int32) * te, pend[-1] - 1)
    tile_expert = jnp.minimum(jnp.sum((tile_start[:, None] >= pend[None, :]).astype(jnp.int32), axis=1),
                              N_EXPERTS - 1).astype(jnp.int32)
    return pos.astype(jnp.int32), tile_expert, n_active


def _dispatch_kernel(pos_ref, x_hbm, xs_in, xs_hbm, sems):
    del xs_in
    sem = sems.at[0]
    tt = pos_ref.shape[-1] // TOP_K
    i = pl.program_id(0)
    base = i * tt

    def body(r, c):
        for k in range(TOP_K):
            pltpu.make_async_copy(x_hbm.at[pl.ds(base + r, 1)],
                                  xs_hbm.at[pl.ds(pos_ref[0, 0, r * TOP_K + k], 1)], sem).start()
        return c

    lax.fori_loop(0, tt, body, 0, unroll=8)

    def wait_tile():
        n = tt * TOP_K
        pltpu.make_async_copy(xs_hbm.at[pl.ds(0, n)], xs_hbm.at[pl.ds(n, n)], sem).wait()

    @pl.when(i > 0)
    def _():
        wait_tile()

    @pl.when(i == pl.num_programs(0) - 1)
    def _():
        wait_tile()


def _dispatch(pos3, xpk, xsort):
    n_tok_tiles = pos3.shape[0]
    return pl.pallas_call(
        _dispatch_kernel,
        grid=(n_tok_tiles,),
        in_specs=[pl.BlockSpec((1, 1, pos3.shape[2]), lambda i: (i, 0, 0), memory_space=pltpu.SMEM),
                  pl.BlockSpec(memory_space=pl.ANY),
                  pl.BlockSpec(memory_space=pl.ANY)],
        out_specs=pl.BlockSpec(memory_space=pl.ANY),
        out_shape=jax.ShapeDtypeStruct(xsort.shape, xsort.dtype),
        scratch_shapes=[pltpu.SemaphoreType.DMA((1,))],
        input_output_aliases={2: 0},
        compiler_params=_params(1),
        name="moe_dispatch",
    )(pos3, xpk, xsort)


def _gmm_kernel(te_ref, na_ref, xs_ref, wg_ref, wu_ref, wd_ref, y_ref, wgb_ref, wub_ref, wdb_ref):
    t = pl.program_id(0)
    n_active = na_ref[0]

    @pl.when(t < n_active)
    def _():
        prev = te_ref[jnp.maximum(t - 1, 0)]

        @pl.when((t == 0) | (te_ref[t] != prev))
        def _():
            wgb_ref[...] = wg_ref[...].astype(jnp.bfloat16)
            wub_ref[...] = wu_ref[...].astype(jnp.bfloat16)
            wdb_ref[...] = wd_ref[...].astype(jnp.bfloat16)

        u = xs_ref[...]
        lo = lax.bitcast_convert_type(u << 16, jnp.float32).astype(jnp.bfloat16)
        hi = lax.bitcast_convert_type(u & jnp.uint32(0xFFFF0000), jnp.float32).astype(jnp.bfloat16)

        def up(w_ref):
            return (jnp.dot(lo, w_ref[:HALF_D, :], preferred_element_type=jnp.float32)
                    + jnp.dot(hi, w_ref[HALF_D:, :], preferred_element_type=jnp.float32))

        hg = up(wgb_ref)
        hu = up(wub_ref)
        act = (hg * (1.0 / (1.0 + jnp.exp(-hg))) * hu).astype(jnp.bfloat16)
        y_ref[...] = jnp.dot(act, wdb_ref[...], preferred_element_type=jnp.float32)

    @pl.when(t >= n_active)
    def _():
        y_ref[...] = jnp.zeros(y_ref.shape, jnp.float32)


def _gmm(tile_expert, n_active, xsort, w_gate, w_up, w_down, layer):
    te = EXPERT_TILE
    n_tiles = xsort.shape[0] // te
    grid_spec = pltpu.PrefetchScalarGridSpec(
        num_scalar_prefetch=2,
        grid=(n_tiles,),
        in_specs=[
            pl.BlockSpec((te, HALF_D), lambda t, e, n: (t, 0)),
            pl.BlockSpec((None, None, D_MODEL, D_EXPERT), lambda t, e, n: (layer, e[t], 0, 0)),
            pl.BlockSpec((None, None, D_MODEL, D_EXPERT), lambda t, e, n: (layer, e[t], 0, 0)),
            pl.BlockSpec((None, None, D_EXPERT, D_MODEL), lambda t, e, n: (layer, e[t], 0, 0)),
        ],
        out_specs=pl.BlockSpec((te, D_MODEL), lambda t, e, n: (t, 0)),
        scratch_shapes=[
            pltpu.VMEM((D_MODEL, D_EXPERT), jnp.bfloat16),
            pltpu.VMEM((D_MODEL, D_EXPERT), jnp.bfloat16),
            pltpu.VMEM((D_EXPERT, D_MODEL), jnp.bfloat16),
        ],
    )
    return pl.pallas_call(
        _gmm_kernel,
        grid_spec=grid_spec,
        out_shape=jax.ShapeDtypeStruct((xsort.shape[0], D_MODEL), jnp.float32),
        compiler_params=_params(1),
        name="moe_grouped",
    )(tile_expert, n_active, xsort, w_gate, w_up, w_down)


def _combine_kernel(pos_ref, posn_ref, x1_ref, route_ref, g_ref, b_ref, y_hbm, x2_ref, x2b_ref,
                    gbuf, sem, *, alpha):
    tt = x1_ref.shape[0]
    i = pl.program_id(0)
    slot = i % 2

    def start(idx_ref, s_):
        def body(r, c):
            for k in range(TOP_K):
                pltpu.make_async_copy(y_hbm.at[pl.ds(idx_ref[0, 0, r * TOP_K + k], 1)],
                                      gbuf.at[s_, k, pl.ds(r, 1)], sem.at[s_]).start()
            return c
        lax.fori_loop(0, tt, body, 0, unroll=8)

    @pl.when(i == 0)
    def _():
        start(pos_ref, 0)

    for k in range(TOP_K):
        pltpu.make_async_copy(y_hbm.at[pl.ds(0, tt)], gbuf.at[slot, k], sem.at[slot]).wait()

    @pl.when(i + 1 < pl.num_programs(0))
    def _():
        start(posn_ref, 1 - slot)

    ff = (route_ref[:, TOP_K:TOP_K + 1] * gbuf[slot, 0]
          + route_ref[:, TOP_K + 1:TOP_K + 2] * gbuf[slot, 1])
    x2 = _layer_norm_rows(alpha * x1_ref[...] + ff, g_ref[...], b_ref[...])
    x2_ref[...] = x2
    x2b_ref[...] = x2.astype(jnp.bfloat16)


def _combine(pos3, x1, route, g2, b2, ysort, alpha):
    rows = x1.shape[0]
    tt = pos3.shape[2] // TOP_K
    n_i = rows // tt
    return pl.pallas_call(
        functools.partial(_combine_kernel, alpha=alpha),
        grid=(n_i,),
        in_specs=[
            pl.BlockSpec((1, 1, tt * TOP_K), lambda i: (i, 0, 0), memory_space=pltpu.SMEM),
            pl.BlockSpec((1, 1, tt * TOP_K), lambda i: (jnp.minimum(i + 1, n_i - 1), 0, 0),
                         memory_space=pltpu.SMEM),
            pl.BlockSpec((tt, D_MODEL), lambda i: (i, 0)),
            pl.BlockSpec((tt, LANES), lambda i: (i, 0)),
            pl.BlockSpec((1, D_MODEL), lambda i: (0, 0)),
            pl.BlockSpec((1, D_MODEL), lambda i: (0, 0)),
            pl.BlockSpec(memory_space=pl.ANY),
        ],
        out_specs=[pl.BlockSpec((tt, D_MODEL), lambda i: (i, 0)),
                   pl.BlockSpec((tt, D_MODEL), lambda i: (i, 0))],
        out_shape=[jax.ShapeDtypeStruct((rows, D_MODEL), jnp.float32),
                   jax.ShapeDtypeStruct((rows, D_MODEL), jnp.bfloat16)],
        scratch_shapes=[pltpu.VMEM((2, TOP_K, tt, D_MODEL), jnp.float32),
                        pltpu.SemaphoreType.DMA((2,))],
        compiler_params=_params(1),
        name="combine_ln",
    )(pos3, pos3, x1, route, g2, b2, ysort)


def kernel(x_prompt, x_sample, cache_k, cache_v, page_table, w_in, b_in, lam_q1, lam_k1, lam_q2, lam_k2, subln_g, sg_ln_g, sg_ln_b, w_sp, b_sp, w_pa, w_pb, w_o, ln1_g, ln1_b, w_rg, b_rg, w_re, b_re, w_gate, w_up, w_down, ln2_g, ln2_b):
    batch, seq, _ = x_prompt.shape
    dec_batch, dec_seq, _ = x_sample.shape
    depth = w_in.shape[0]
    n_prompt = batch * seq
    n_sample = dec_batch * dec_seq
    assert seq % ATT_BLOCK == 0 and n_prompt % PROJ_ROW_TILE == 0 and seq % CHUNK == 0
    assert n_sample <= CHUNK and n_sample % 16 == 0 and page_table.shape[1] % PAGES_PER_STEP == 0
    assert dec_seq & (dec_seq - 1) == 0 and dec_seq <= NEW_TOKEN_ROWS
    alpha = (2 * depth) ** 0.25
    bf16 = jnp.bfloat16
    f32 = jnp.float32
    te = EXPERT_TILE
    n_assign = (n_prompt + n_sample) * TOP_K
    n_tiles = -(-n_assign // te) + N_EXPERTS

    xp = x_prompt.reshape(n_prompt, D_MODEL)
    xs = x_sample.reshape(n_sample, D_MODEL)
    xpb = xp.astype(bf16)
    xsb = xs.astype(bf16)
    k_stack = jnp.zeros((depth, n_prompt, N_HEADS, KEY_DIM), f32)
    v_stack = jnp.zeros((depth, n_prompt, N_HEADS, VAL_DIM), f32)
    xsort = jnp.zeros((n_tiles * te, HALF_D), jnp.uint32)

    ridx = jnp.arange(n_sample)
    rt = ridx % dec_seq
    rs = ridx // dec_seq
    smask = ((rt[None, :] <= rt[:, None]) & (rs[None, :] == rs[:, None])).astype(f32)
    tril = jnp.tril(jnp.ones((CHUNK, CHUNK), f32))
    eye_m = jnp.eye(2, dtype=f32)

    ks_rows, vs_rows, sgv_rows = [], [], []
    for l in range(depth):
        lam_i = _lambda_init(l)
        w_in_b = w_in[l].astype(bf16)
        b_row = b_in[l].reshape(1, -1)
        sg_g = sg_ln_g[l].reshape(1, -1)
        sg_b = sg_ln_b[l].reshape(1, -1)
        sg_off = 3 * Q_W
        gate_off = sg_off + 2 * SG_WIDTH
        qkv_p, k_stack, v_stack = _proj_qkv_rows(xpb, w_in_b, b_row, k_stack, v_stack, l)
        qkv_s = _proj(_proj_qkv_kernel, xsb, w_in_b, b_row, 0, 3 * Q_W, f32, name="proj_qkv_s")
        sg_p = _proj(_proj_sg_kernel, xpb, w_in_b, b_row, sg_off, 2 * SG_WIDTH, f32, (sg_g, sg_b), "proj_sg")
        sg_s = _proj(_proj_sg_kernel, xsb, w_in_b, b_row, sg_off, 2 * SG_WIDTH, f32, (sg_g, sg_b), "proj_sg_s")
        gate_p = _proj(_proj_gate_kernel, xpb, w_in_b, b_row, gate_off, 2 * D_MODEL, bf16, name="proj_gate")
        gate_s = _proj(_proj_gate_kernel, xsb, w_in_b, b_row, gate_off, 2 * D_MODEL, bf16, name="proj_gate_s")

        lamp = jnp.stack([lam_q1[l], lam_k1[l], lam_q2[l], lam_k2[l]]).astype(f32)
        ao_p = _flash(qkv_p, lamp, subln_g[l].reshape(VAL_DIM, 1), lam_i, batch, seq)

        q5 = qkv_s[:, :Q_W].reshape(dec_batch, dec_seq, N_HEADS, 2, HEAD_DIM)
        qbig = jnp.einsum("bthmd,mn->bmthnd", q5, eye_m).reshape(
            dec_batch, 2 * dec_seq * N_HEADS, KEY_DIM).astype(bf16)
        k_new = qkv_s[:, Q_W:2 * Q_W].reshape(dec_batch, dec_seq, N_HEADS, KEY_DIM)
        v_new = qkv_s[:, 2 * Q_W:].reshape(dec_batch, dec_seq, N_HEADS, VAL_DIM)
        pad_new = ((0, 0), (0, NEW_TOKEN_ROWS - dec_seq), (0, 0), (0, 0))
        ao_s = _decode(page_table, qbig, cache_k, cache_v, l, jnp.pad(k_new, pad_new), jnp.pad(v_new, pad_new),
                       lamp, subln_g[l].reshape(1, -1), lam_i, dec_seq)
        ao_s = ao_s.reshape(n_sample, N_HEADS * VAL_DIM).astype(bf16)

        wmix_p = (w_sp[l] * tril).astype(bf16)
        wmix_s = (w_sp[l][:, rt[:, None], rt[None, :]] * smask).astype(bf16)
        bmix_p = jnp.repeat(b_sp[l].T, SG_CH, axis=1)
        bmix_s = bmix_p[rt]
        wr = jnp.concatenate([w_rg[l], w_re[l],
                              jnp.zeros((D_MODEL, LANES - N_EGROUPS - N_EXPERTS), f32)], axis=1).astype(bf16)
        br = jnp.concatenate([b_rg[l], b_re[l], jnp.zeros((LANES - N_EGROUPS - N_EXPERTS,), f32)]).reshape(1, -1)
        wts = (w_pa[l].astype(bf16), w_pb[l].astype(bf16), w_o[l].astype(bf16),
               ln1_g[l].reshape(1, -1), ln1_b[l].reshape(1, -1), wr, br)
        x1_p, xpk_p, route_p = _merge(ao_p, sg_p, gate_p, xp, wmix_p, bmix_p, *wts, alpha)
        x1_s, xpk_s, route_s = _merge(ao_s, sg_s, gate_s, xs, wmix_s, bmix_s, *wts, alpha)

        pos, tile_expert, n_active = _moe_plan(route_p, route_s, n_tiles)
        pos_p = pos[:n_prompt * TOP_K].reshape(n_prompt // ROW_TILE, 1, ROW_TILE * TOP_K)
        pos_s = pos[n_prompt * TOP_K:].reshape(1, 1, n_sample * TOP_K)
        xsort = _dispatch(pos_p, xpk_p, xsort)
        xsort = _dispatch(pos_s, xpk_s, xsort)
        ysort = _gmm(tile_expert, n_active, xsort, w_gate, w_up, w_down, l)
        g2 = ln2_g[l].reshape(1, -1)
        b2 = ln2_b[l].reshape(1, -1)
        xp, xpb = _combine(pos_p, x1_p, route_p, g2, b2, ysort, alpha)
        xs, xsb = _combine(pos_s, x1_s, route_s, g2, b2, ysort, alpha)

        ks_rows.append(k_new)
        vs_rows.append(v_new)
        sgv_rows.append(sg_s[:, SG_WIDTH:].reshape(dec_batch, dec_seq, SG_WIDTH))

    yp = xp.reshape(batch, seq, D_MODEL)
    ys = xs.reshape(dec_batch, dec_seq, D_MODEL)
    kv_shape = (depth, batch, seq, N_HEADS, KEY_DIM)
    return (yp, ys, k_stack.reshape(kv_shape), v_stack.reshape(kv_shape), jnp.stack(ks_rows),
            jnp.stack(vs_rows), jnp.stack(sgv_rows))
```

```python
import functools
import math

import jax
import jax.numpy as jnp
from jax import lax
from jax.experimental import pallas as pl
from jax.experimental.pallas import tpu as pltpu

D_MODEL = 2048
N_HEADS = 8
HEAD_DIM = 64
KEY_DIM = 2 * HEAD_DIM
VAL_DIM = 2 * HEAD_DIM
Q_W = N_HEADS * KEY_DIM
SG_WIDTH = 1024
SG_GROUPS = 8
SG_CH = SG_WIDTH // SG_GROUPS
CHUNK = 128
N_EGROUPS = 4
EXP_PER_GROUP = 8
N_EXPERTS = N_EGROUPS * EXP_PER_GROUP
TOP_K = 2
D_EXPERT = 512
LN_EPS = 1e-5
NEG = -1e30
HEAD_SHIFT = N_HEADS.bit_length() - 1
assert 1 << HEAD_SHIFT == N_HEADS
HALF_D = D_MODEL // 2
SCORE_SCALE = HEAD_DIM ** -0.5 * math.log2(math.e)
ONES_ROWS = 16

LANES = 128
COL_BLOCK = 1024
ROW_TILE = 256
PROJ_ROW_TILE = 1024
ATT_BLOCK = 512
ATT_COLS = 512
PAGES_PER_STEP = 8
NEW_TOKEN_ROWS = 16
EXPERT_TILE = 256
VMEM_LIMIT = 56 * 1024 * 1024


def _lambda_init(layer_idx):
    return 0.8 - 0.6 * math.exp(-0.3 * layer_idx)


def _layer_norm_rows(x, g, b):
    mu = jnp.mean(x, axis=-1, keepdims=True)
    xc = x - mu
    var = jnp.mean(xc * xc, axis=-1, keepdims=True)
    return xc * lax.rsqrt(var + LN_EPS) * g + b


def _lam_value(lamp_ref, lam_i):
    a = jnp.sum(lamp_ref[0:1, :] * lamp_ref[1:2, :], axis=-1, keepdims=True)
    b = jnp.sum(lamp_ref[2:3, :] * lamp_ref[3:4, :], axis=-1, keepdims=True)
    return jnp.exp(a) - jnp.exp(b) + lam_i


def _params(n_axes):
    return pltpu.CompilerParams(dimension_semantics=("arbitrary",) * n_axes, vmem_limit_bytes=VMEM_LIMIT)


def _qkv_block(x_ref, w_ref, b_ref):
    z = jnp.dot(x_ref[...], w_ref[...], preferred_element_type=jnp.float32) + b_ref[...]
    scale = jnp.where(pl.program_id(0) == 0, SCORE_SCALE, 1.0).astype(jnp.float32)
    return z * scale


def _proj_qkv_kernel(x_ref, w_ref, b_ref, o_ref):
    o_ref[...] = _qkv_block(x_ref, w_ref, b_ref)


def _proj_qkv_rows_kernel(x_ref, w_ref, b_ref, kin_ref, vin_ref, o_ref, ko_ref, vo_ref):
    del kin_ref, vin_ref
    z = _qkv_block(x_ref, w_ref, b_ref)
    o_ref[...] = z
    j = pl.program_id(0)

    def per_head(dst_ref):
        for h in range(N_HEADS):
            dst_ref[:, h, :] = z[:, h * KEY_DIM:(h + 1) * KEY_DIM]

    @pl.when(j == 1)
    def _():
        per_head(ko_ref)

    @pl.when(j == 2)
    def _():
        per_head(vo_ref)


def _proj_sg_kernel(x_ref, w_ref, b_ref, g_ref, bb_ref, o_ref):
    z = jnp.dot(x_ref[...], w_ref[...], preferred_element_type=jnp.float32) + b_ref[...]
    y = jax.nn.gelu(z)
    j = pl.program_id(0)

    @pl.when(j == 0)
    def _():
        o_ref[...] = y

    @pl.when(j == 1)
    def _():
        o_ref[...] = _layer_norm_rows(y, g_ref[...], bb_ref[...])


def _proj_gate_kernel(x_ref, w_ref, b_ref, o_ref):
    z = jnp.dot(x_ref[...], w_ref[...], preferred_element_type=jnp.float32) + b_ref[...]
    o_ref[...] = (1.0 / (1.0 + jnp.exp(-z))).astype(o_ref.dtype)


def _proj_in_specs(tm, off):
    return [
        pl.BlockSpec((tm, D_MODEL), lambda j, i: (i, 0)),
        pl.BlockSpec((D_MODEL, COL_BLOCK), lambda j, i: (0, off + j)),
        pl.BlockSpec((1, COL_BLOCK), lambda j, i: (0, off + j)),
    ]


def _proj(body, xb, w, b, col_off, n_col, out_dtype, extra=(), name=None):
    rows = xb.shape[0]
    tm = min(PROJ_ROW_TILE, rows)
    extra_specs = [pl.BlockSpec((1, COL_BLOCK), lambda j, i: (0, 0)) for _ in extra]
    return pl.pallas_call(
        body,
        grid=(n_col // COL_BLOCK, rows // tm),
        in_specs=_proj_in_specs(tm, col_off // COL_BLOCK) + extra_specs,
        out_specs=pl.BlockSpec((tm, COL_BLOCK), lambda j, i: (i, j)),
        out_shape=jax.ShapeDtypeStruct((rows, n_col), out_dtype),
        compiler_params=_params(2),
        name=name,
    )(xb, w, b, *extra)


def _proj_qkv_rows(xb, w, b, k_stack, v_stack, layer):
    rows = xb.shape[0]
    tm = PROJ_ROW_TILE
    n_i = rows // tm

    def k_rows(j, i):
        return (layer, jnp.where(j < 1, 0, jnp.where(j == 1, i, n_i - 1)), 0, 0)

    def v_rows(j, i):
        return (layer, jnp.where(j < 2, 0, i), 0, 0)

    any_spec = pl.BlockSpec(memory_space=pl.ANY)
    return pl.pallas_call(
        _proj_qkv_rows_kernel,
        grid=(3, n_i),
        in_specs=_proj_in_specs(tm, 0) + [any_spec, any_spec],
        out_specs=[pl.BlockSpec((tm, COL_BLOCK), lambda j, i: (i, j)),
                   pl.BlockSpec((None, tm, N_HEADS, KEY_DIM), k_rows),
                   pl.BlockSpec((None, tm, N_HEADS, VAL_DIM), v_rows)],
        out_shape=[jax.ShapeDtypeStruct((rows, 3 * Q_W), jnp.float32),
                   jax.ShapeDtypeStruct(k_stack.shape, k_stack.dtype),
                   jax.ShapeDtypeStruct(v_stack.shape, v_stack.dtype)],
        input_output_aliases={3: 1, 4: 2},
        compiler_params=_params(2),
        name="proj_qkv_rows",
    )(xb, w, b, k_stack, v_stack)


def _flash_kernel(q_ref, k_ref, v_ref, lamp_ref, g_ref, o_ref,
                  kb_ref, vt_ref, qs_ref, m_ref, acc_ref, *, lam_i, n_blk):
    tq = ATT_BLOCK
    qi = pl.program_id(1)

    @pl.when(qi == 0)
    def _():
        for c in range(n_blk):
            rows = slice(c * tq, (c + 1) * tq)
            kb_ref[c] = k_ref[rows, :].astype(jnp.bfloat16)
            vt_ref[c, :VAL_DIM, :] = v_ref[rows, :].T.astype(jnp.bfloat16)
            vt_ref[c, VAL_DIM:, :] = jnp.ones((ONES_ROWS, tq), jnp.bfloat16)

    qt = q_ref[...].T
    row = lax.broadcasted_iota(jnp.int32, qt.shape, 0)
    qs_ref[:, :tq] = jnp.where(row < HEAD_DIM, qt, 0.0).astype(jnp.bfloat16)
    qs_ref[:, tq:] = jnp.where(row >= HEAD_DIM, qt, 0.0).astype(jnp.bfloat16)
    m_ref[...] = jnp.full(m_ref.shape, NEG, jnp.float32)
    acc_ref[...] = jnp.zeros(acc_ref.shape, jnp.float32)

    def step(ki, masked):
        kblk = kb_ref[ki]
        vblk = vt_ref[ki]
        n_chains = 2 * tq // ATT_COLS

        def scores(c):
            cols = slice(c * ATT_COLS, (c + 1) * ATT_COLS)
            s = jnp.dot(kblk, qs_ref[:, cols], preferred_element_type=jnp.float32)
            if masked:
                kpos = lax.broadcasted_iota(jnp.int32, s.shape, 0)
                qpos = lax.broadcasted_iota(jnp.int32, s.shape, 1) + (c * ATT_COLS) % tq
                s = jnp.where(kpos <= qpos, s, NEG)
            return s

        new_state = []
        s_next = scores(0)
        for c in range(n_chains):
            cols = slice(c * ATT_COLS, (c + 1) * ATT_COLS)
            s = s_next
            if c + 1 < n_chains:
                s_next = scores(c + 1)
            m_prev = m_ref[:, cols]
            m_cur = jnp.maximum(m_prev, jnp.max(s, axis=0, keepdims=True))
            alpha = jnp.exp2(m_prev - m_cur)
            p = jnp.exp2(s - m_cur)
            acc_new = alpha * acc_ref[:, cols] + jnp.dot(
                vblk, p.astype(jnp.bfloat16), preferred_element_type=jnp.float32)
            new_state.append((cols, m_cur, acc_new))
        for cols, m_cur, acc_new in new_state:
            m_ref[:, cols] = m_cur
            acc_ref[:, cols] = acc_new

    def body(ki, carry):
        step(ki, False)
        return carry

    lax.fori_loop(0, qi, body, 0)
    step(qi, True)

    lam = _lam_value(lamp_ref, lam_i)
    l_row = acc_ref[VAL_DIM:VAL_DIM + 1, :]
    o = (acc_ref[:VAL_DIM, :tq] / l_row[:, :tq]) - lam * (acc_ref[:VAL_DIM, tq:] / l_row[:, tq:])
    ms = jnp.mean(o * o, axis=0, keepdims=True)
    r = o * lax.rsqrt(ms + LN_EPS) * g_ref[...] * (1.0 - lam_i)
    o_ref[...] = r.T.astype(o_ref.dtype)


def _flash(qkv, lamp, g_col, lam_i, batch, seq):
    tq = ATT_BLOCK
    n_blk = seq // tq
    kcol = Q_W // KEY_DIM
    vcol = 2 * Q_W // KEY_DIM
    return pl.pallas_call(
        functools.partial(_flash_kernel, lam_i=lam_i, n_blk=n_blk),
        grid=(batch * N_HEADS, n_blk),
        in_specs=[
            pl.BlockSpec((tq, KEY_DIM), lambda bh, qi: ((bh // N_HEADS) * n_blk + qi, bh % N_HEADS)),
            pl.BlockSpec((seq, KEY_DIM), lambda bh, qi: (bh // N_HEADS, kcol + bh % N_HEADS)),
            pl.BlockSpec((seq, VAL_DIM), lambda bh, qi: (bh // N_HEADS, vcol + bh % N_HEADS)),
            pl.BlockSpec((4, HEAD_DIM), lambda bh, qi: (0, 0)),
            pl.BlockSpec((VAL_DIM, 1), lambda bh, qi: (0, 0)),
        ],
        out_specs=pl.BlockSpec((tq, VAL_DIM), lambda bh, qi: ((bh // N_HEADS) * n_blk + qi, bh % N_HEADS)),
        out_shape=jax.ShapeDtypeStruct((batch * seq, N_HEADS * VAL_DIM), jnp.bfloat16),
        scratch_shapes=[
            pltpu.VMEM((n_blk, tq, KEY_DIM), jnp.bfloat16),
            pltpu.VMEM((n_blk, VAL_DIM + ONES_ROWS, tq), jnp.bfloat16),
            pltpu.VMEM((KEY_DIM, 2 * tq), jnp.bfloat16),
            pltpu.VMEM((1, 2 * tq), jnp.float32),
            pltpu.VMEM((VAL_DIM + ONES_ROWS, 2 * tq), jnp.float32),
        ],
        compiler_params=_params(2),
        name="flash_diff_attn",
    )(qkv, qkv, qkv, lamp, g_col)


def _decode_kernel(pt_ref, qbig_ref, *refs, lam_i, dec_seq):
    del pt_ref
    pp = PAGES_PER_STEP
    k_refs = refs[:pp]
    v_refs = refs[pp:2 * pp]
    kn_ref, vn_ref, lamp_ref, g_ref, o_ref, m_ref, l_ref, acc_ref = refs[2 * pp:]
    p_id = pl.program_id(1)
    nq = qbig_ref.shape[0]
    half = nq // 2

    @pl.when(p_id == 0)
    def _():
        m_ref[...] = jnp.full(m_ref.shape, NEG, jnp.float32)
        l_ref[...] = jnp.zeros(l_ref.shape, jnp.float32)
        acc_ref[...] = jnp.zeros(acc_ref.shape, jnp.float32)

    def flat(refs_, width):
        parts = [r[...].reshape(r.shape[0] * N_HEADS, width) for r in refs_]
        rows = parts[0] if len(parts) == 1 else jnp.concatenate(parts, axis=0)
        return rows.astype(jnp.bfloat16)

    def update(k2, v2, causal):
        s = lax.dot_general(qbig_ref[...], k2, (((1,), (1,)), ((), ())),
                            preferred_element_type=jnp.float32)
        qrow = lax.broadcasted_iota(jnp.int32, s.shape, 0)
        kcol = lax.broadcasted_iota(jnp.int32, s.shape, 1)
        ok = (qrow & (N_HEADS - 1)) == (kcol & (N_HEADS - 1))
        if causal:
            q_t = (qrow >> HEAD_SHIFT) & (dec_seq - 1)
            ok = ok & ((kcol >> HEAD_SHIFT) <= q_t)
        s = jnp.where(ok, s, NEG)
        m_prev = m_ref[...]
        m_cur = jnp.maximum(m_prev, jnp.max(s, axis=-1, keepdims=True))
        alpha = jnp.exp2(m_prev - m_cur)
        p = jnp.exp2(s - m_cur)
        l_ref[...] = alpha * l_ref[...] + jnp.sum(p, axis=-1, keepdims=True)
        acc_ref[...] = alpha * acc_ref[...] + jnp.dot(
            p.astype(jnp.bfloat16), v2, preferred_element_type=jnp.float32)
        m_ref[...] = m_cur

    update(flat(k_refs, KEY_DIM), flat(v_refs, VAL_DIM), False)

    @pl.when(p_id == pl.num_programs(1) - 1)
    def _():
        update(flat([kn_ref], KEY_DIM), flat([vn_ref], VAL_DIM), True)
        lam = _lam_value(lamp_ref, lam_i)
        o1 = acc_ref[0:half, :] / l_ref[0:half, :]
        o2 = acc_ref[half:nq, :] / l_ref[half:nq, :]
        d = o1 - lam * o2
        ms = jnp.mean(d * d, axis=-1, keepdims=True)
        o_ref[...] = d * lax.rsqrt(ms + LN_EPS) * g_ref[...] * (1.0 - lam_i)


def _decode(page_table, qbig, cache_k, cache_v, layer, kn, vn, lamp, g_row, lam_i, dec_seq):
    dec_batch, n_pages = page_table.shape
    pp = PAGES_PER_STEP
    page = cache_k.shape[2]
    nq = qbig.shape[1]
    new_rows = kn.shape[1]

    def page_spec(r):
        return pl.BlockSpec((None, None, page, N_HEADS, KEY_DIM),
                            lambda b, p, pt: (layer, pt[b, p * pp + r], 0, 0, 0))

    grid_spec = pltpu.PrefetchScalarGridSpec(
        num_scalar_prefetch=1,
        grid=(dec_batch, n_pages // pp),
        in_specs=[pl.BlockSpec((None, nq, KEY_DIM), lambda b, p, pt: (b, 0, 0))]
        + [page_spec(r) for r in range(pp)] + [page_spec(r) for r in range(pp)]
        + [pl.BlockSpec((None, new_rows, N_HEADS, KEY_DIM), lambda b, p, pt: (b, 0, 0, 0)),
           pl.BlockSpec((None, new_rows, N_HEADS, VAL_DIM), lambda b, p, pt: (b, 0, 0, 0)),
           pl.BlockSpec((4, HEAD_DIM), lambda b, p, pt: (0, 0)),
           pl.BlockSpec((1, VAL_DIM), lambda b, p, pt: (0, 0))],
        out_specs=pl.BlockSpec((None, nq // 2, VAL_DIM), lambda b, p, pt: (b, 0, 0)),
        scratch_shapes=[
            pltpu.VMEM((nq, 1), jnp.float32),
            pltpu.VMEM((nq, 1), jnp.float32),
            pltpu.VMEM((nq, VAL_DIM), jnp.float32),
        ],
    )
    return pl.pallas_call(
        functools.partial(_decode_kernel, lam_i=lam_i, dec_seq=dec_seq),
        grid_spec=grid_spec,
        out_shape=jax.ShapeDtypeStruct((dec_batch, nq // 2, VAL_DIM), jnp.float32),
        compiler_params=_params(2),
        name="paged_diff_attn",
    )(page_table, qbig, *([cache_k] * pp), *([cache_v] * pp), kn, vn, lamp, g_row)


def _merge_kernel(ao_ref, sg_ref, gate_ref, x_ref, wmix_ref, bmix_ref, wpa_ref, wpb_ref, wo_ref,
                  g1_ref, b1_ref, wr_ref, br_ref, x1_ref, xpk_ref, route_ref, so_ref, *, alpha):
    tm = x_ref.shape[0]
    ch = wmix_ref.shape[-1]
    for c in range(tm // ch):
        rows = slice(c * ch, (c + 1) * ch)
        for g in range(SG_GROUPS):
            cols = slice(g * SG_CH, (g + 1) * SG_CH)
            vs = sg_ref[rows, SG_WIDTH + g * SG_CH:SG_WIDTH + (g + 1) * SG_CH].astype(jnp.bfloat16)
            mixed = jnp.dot(wmix_ref[g], vs, preferred_element_type=jnp.float32) + bmix_ref[:, cols]
            so_ref[rows, cols] = (sg_ref[rows, cols] * mixed).astype(jnp.bfloat16)

    a = jnp.dot(ao_ref[...], wpa_ref[...], preferred_element_type=jnp.float32)
    bm = jnp.dot(so_ref[...], wpb_ref[...], preferred_element_type=jnp.float32)
    merged = gate_ref[:, :D_MODEL].astype(jnp.float32) * a + gate_ref[:, D_MODEL:].astype(jnp.float32) * bm
    y = jnp.dot(merged.astype(jnp.bfloat16), wo_ref[...], preferred_element_type=jnp.float32)
    x1 = _layer_norm_rows(alpha * x_ref[...] + y, g1_ref[...], b1_ref[...])
    x1_ref[...] = x1

    x1b = x1.astype(jnp.bfloat16)
    lo = lax.bitcast_convert_type(x1b[:, :HALF_D].astype(jnp.float32), jnp.uint32)
    hi = lax.bitcast_convert_type(x1b[:, HALF_D:].astype(jnp.float32), jnp.uint32)
    xpk_ref[...] = (lo >> 16) | (hi & jnp.uint32(0xFFFF0000))

    logits = jnp.dot(x1b, wr_ref[...], preferred_element_type=jnp.float32) + br_ref[...]
    lane = lax.broadcasted_iota(jnp.int32, logits.shape, 1)
    ninf = -jnp.inf
    gmask = lane < N_EGROUPS
    gl = jnp.where(gmask, logits, ninf)
    gmax = jnp.max(gl, axis=-1, keepdims=True)
    gi = jnp.min(jnp.where(gl == gmax, lane, LANES), axis=-1, keepdims=True)
    gsum = jnp.sum(jnp.where(gmask, jnp.exp(gl - gmax), 0.0), axis=-1, keepdims=True)
    gw = 1.0 / gsum
    emask = (lane >= N_EGROUPS) & (lane < N_EGROUPS + N_EXPERTS) & (((lane - N_EGROUPS) // EXP_PER_GROUP) == gi)
    el = jnp.where(emask, logits, ninf)
    v1 = jnp.max(el, axis=-1, keepdims=True)
    i1 = jnp.min(jnp.where(el == v1, lane, LANES), axis=-1, keepdims=True)
    el2 = jnp.where(lane == i1, ninf, el)
    v2 = jnp.max(el2, axis=-1, keepdims=True)
    i2 = jnp.min(jnp.where(el2 == v2, lane, LANES), axis=-1, keepdims=True)
    e2 = jnp.exp(v2 - v1)
    den = 1.0 + e2
    tw1 = (1.0 / den) * gw
    tw2 = (e2 / den) * gw
    id1 = (i1 - N_EGROUPS).astype(jnp.float32)
    id2 = (i2 - N_EGROUPS).astype(jnp.float32)
    route_ref[...] = jnp.where(lane == 0, id1, jnp.where(lane == 1, id2,
                               jnp.where(lane == 2, tw1, jnp.where(lane == 3, tw2, 0.0))))


def _merge(ao, sg, gate, x, wmix, bmix, wpa, wpb, wo, g1, b1, wr, br, alpha):
    rows = x.shape[0]
    tm = min(ROW_TILE, rows)

    def full(arr):
        nd = arr.ndim
        return pl.BlockSpec(arr.shape, lambda i, _nd=nd: (0,) * _nd, pipeline_mode=pl.Buffered(1))

    return pl.pallas_call(
        functools.partial(_merge_kernel, alpha=alpha),
        grid=(rows // tm,),
        in_specs=[
            pl.BlockSpec((tm, ao.shape[1]), lambda i: (i, 0)),
            pl.BlockSpec((tm, sg.shape[1]), lambda i: (i, 0)),
            pl.BlockSpec((tm, gate.shape[1]), lambda i: (i, 0)),
            pl.BlockSpec((tm, D_MODEL), lambda i: (i, 0)),
            full(wmix), full(bmix), full(wpa), full(wpb), full(wo), full(g1), full(b1), full(wr), full(br),
        ],
        out_specs=[pl.BlockSpec((tm, D_MODEL), lambda i: (i, 0)),
                   pl.BlockSpec((tm, HALF_D), lambda i: (i, 0)),
                   pl.BlockSpec((tm, LANES), lambda i: (i, 0))],
        out_shape=[jax.ShapeDtypeStruct((rows, D_MODEL), jnp.float32),
                   jax.ShapeDtypeStruct((rows, HALF_D), jnp.uint32),
                   jax.ShapeDtypeStruct((rows, LANES), jnp.float32)],
        scratch_shapes=[pltpu.VMEM((tm, SG_WIDTH), jnp.bfloat16)],
        compiler_params=_params(1),
        name="merge_ln_router",
    )(ao, sg, gate, x, wmix, bmix, wpa, wpb, wo, g1, b1, wr, br)


def _moe_plan(route_p, route_s, n_tiles):
    te = EXPERT_TILE
    ef = jnp.concatenate([route_p[:, :TOP_K], route_s[:, :TOP_K]], axis=0).astype(jnp.int32).reshape(-1)
    onehot = (ef[:, None] == jnp.arange(N_EXPERTS, dtype=jnp.int32)[None, :]).astype(jnp.int32)
    csum = jnp.cumsum(onehot, axis=0)
    counts = csum[-1]
    rank = jnp.sum(csum * onehot, axis=1) - 1
    pcounts = ((counts + te - 1) // te) * te
    pend = jnp.cumsum(pcounts)
    pstart = pend - pcounts
    pos = jnp.sum(onehot * pstart[None, :], axis=1) + rank
    n_active = (pend[-1] // te).astype(jnp.int32).reshape(1)
    tile_start = jnp.minimum(jnp.arange(n_tiles, dtype=jnp.int32) * te, pend[-1] - 1)
    tile_expert = jnp.minimum(jnp.sum((tile_start[:, None] >= pend[None, :]).astype(jnp.int32), axis=1),
                              N_EXPERTS - 1).astype(jnp.int32)
    return pos.astype(jnp.int32), tile_expert, n_active


def _dispatch_kernel(pos_ref, x_ref, xs_in, xs_hbm, xbuf, sems):
    del xs_in
    tt = x_ref.shape[0]
    i = pl.program_id(0)
    slot = i % 2

    def wait_tile(s):
        for _ in range(TOP_K):
            pltpu.make_async_copy(xbuf.at[s], xs_hbm.at[pl.ds(0, tt)], sems.at[s]).wait()

    @pl.when(i >= 2)
    def _():
        wait_tile(slot)

    xbuf[slot] = x_ref[...]

    def body(r, c):
        for k in range(TOP_K):
            pltpu.make_async_copy(xbuf.at[slot, pl.ds(r, 1)],
                                  xs_hbm.at[pl.ds(pos_ref[0, 0, r * TOP_K + k], 1)], sems.at[slot]).start()
        return c

    lax.fori_loop(0, tt, body, 0, unroll=8)

    n = pl.num_programs(0)

    @pl.when(i == n - 1)
    def _():
        wait_tile(slot)

        @pl.when(n >= 2)
        def _():
            wait_tile(1 - slot)


def _dispatch(pos3, xpk, xsort):
    n_tok_tiles = pos3.shape[0]
    tt = pos3.shape[2] // TOP_K
    return pl.pallas_call(
        _dispatch_kernel,
        grid=(n_tok_tiles,),
        in_specs=[pl.BlockSpec((1, 1, pos3.shape[2]), lambda i: (i, 0, 0), memory_space=pltpu.SMEM),
                  pl.BlockSpec((tt, HALF_D), lambda i: (i, 0)),
                  pl.BlockSpec(memory_space=pl.ANY)],
        out_specs=pl.BlockSpec(memory_space=pl.ANY),
        out_shape=jax.ShapeDtypeStruct(xsort.shape, xsort.dtype),
        scratch_shapes=[pltpu.VMEM((2, tt, HALF_D), jnp.uint32), pltpu.SemaphoreType.DMA((2,))],
        input_output_aliases={2: 0},
        compiler_params=_params(1),
        name="moe_dispatch",
    )(pos3, xpk, xsort)


def _gmm_kernel(te_ref, na_ref, xs_ref, wg_ref, wu_ref, wd_ref, y_ref, wgb_ref, wub_ref, wdb_ref):
    t = pl.program_id(0)
    n_active = na_ref[0]

    @pl.when(t < n_active)
    def _():
        prev = te_ref[jnp.maximum(t - 1, 0)]

        @pl.when((t == 0) | (te_ref[t] != prev))
        def _():
            wgb_ref[...] = wg_ref[...].astype(jnp.bfloat16)
            wub_ref[...] = wu_ref[...].astype(jnp.bfloat16)
            wdb_ref[...] = wd_ref[...].astype(jnp.bfloat16)

        u = xs_ref[...]
        lo = lax.bitcast_convert_type(u << 16, jnp.float32).astype(jnp.bfloat16)
        hi = lax.bitcast_convert_type(u & jnp.uint32(0xFFFF0000), jnp.float32).astype(jnp.bfloat16)

        def up(w_ref):
            return (jnp.dot(lo, w_ref[:HALF_D, :], preferred_element_type=jnp.float32)
                    + jnp.dot(hi, w_ref[HALF_D:, :], preferred_element_type=jnp.float32))

        hg = up(wgb_ref)
        hu = up(wub_ref)
        act = (hg * (1.0 / (1.0 + jnp.exp(-hg))) * hu).astype(jnp.bfloat16)
        y_ref[...] = jnp.dot(act, wdb_ref[...], preferred_element_type=jnp.float32)

    @pl.when(t >= n_active)
    def _():
        y_ref[...] = jnp.zeros(y_ref.shape, jnp.float32)


def _gmm(tile_expert, n_active, xsort, w_gate, w_up, w_down, layer):
    te = EXPERT_TILE
    n_tiles = xsort.shape[0] // te
    grid_spec = pltpu.PrefetchScalarGridSpec(
        num_scalar_prefetch=2,
        grid=(n_tiles,),
        in_specs=[
            pl.BlockSpec((te, HALF_D), lambda t, e, n: (t, 0)),
            pl.BlockSpec((None, None, D_MODEL, D_EXPERT), lambda t, e, n: (layer, e[t], 0, 0)),
            pl.BlockSpec((None, None, D_MODEL, D_EXPERT), lambda t, e, n: (layer, e[t], 0, 0)),
            pl.BlockSpec((None, None, D_EXPERT, D_MODEL), lambda t, e, n: (layer, e[t], 0, 0)),
        ],
        out_specs=pl.BlockSpec((te, D_MODEL), lambda t, e, n: (t, 0)),
        scratch_shapes=[
            pltpu.VMEM((D_MODEL, D_EXPERT), jnp.bfloat16),
            pltpu.VMEM((D_MODEL, D_EXPERT), jnp.bfloat16),
            pltpu.VMEM((D_EXPERT, D_MODEL), jnp.bfloat16),
        ],
    )
    return pl.pallas_call(
        _gmm_kernel,
        grid_spec=grid_spec,
        out_shape=jax.ShapeDtypeStruct((xsort.shape[0], D_MODEL), jnp.float32),
        compiler_params=_params(1),
        name="moe_grouped",
    )(tile_expert, n_active, xsort, w_gate, w_up, w_down)


def _combine_kernel(pos_ref, posn_ref, x1_ref, route_ref, g_ref, b_ref, y_hbm, x2_ref, x2b_ref,
                    gbuf, sem, *, alpha):
    tt = x1_ref.shape[0]
    i = pl.program_id(0)
    slot = i % 2

    def start(idx_ref, s_):
        def body(r, c):
            for k in range(TOP_K):
                pltpu.make_async_copy(y_hbm.at[pl.ds(idx_ref[0, 0, r * TOP_K + k], 1)],
                                      gbuf.at[s_, k, pl.ds(r, 1)], sem.at[s_]).start()
            return c
        lax.fori_loop(0, tt, body, 0, unroll=8)

    @pl.when(i == 0)
    def _():
        start(pos_ref, 0)

    for k in range(TOP_K):
        pltpu.make_async_copy(y_hbm.at[pl.ds(0, tt)], gbuf.at[slot, k], sem.at[slot]).wait()

    @pl.when(i + 1 < pl.num_programs(0))
    def _():
        start(posn_ref, 1 - slot)

    ff = (route_ref[:, TOP_K:TOP_K + 1] * gbuf[slot, 0]
          + route_ref[:, TOP_K + 1:TOP_K + 2] * gbuf[slot, 1])
    x2 = _layer_norm_rows(alpha * x1_ref[...] + ff, g_ref[...], b_ref[...])
    x2_ref[...] = x2
    x2b_ref[...] = x2.astype(jnp.bfloat16)


def _combine(pos3, x1, route, g2, b2, ysort, alpha):
    rows = x1.shape[0]
    tt = pos3.shape[2] // TOP_K
    n_i = rows // tt
    return pl.pallas_call(
        functools.partial(_combine_kernel, alpha=alpha),
        grid=(n_i,),
        in_specs=[
            pl.BlockSpec((1, 1, tt * TOP_K), lambda i: (i, 0, 0), memory_space=pltpu.SMEM),
            pl.BlockSpec((1, 1, tt * TOP_K), lambda i: (jnp.minimum(i + 1, n_i - 1), 0, 0),
                         memory_space=pltpu.SMEM),
            pl.BlockSpec((tt, D_MODEL), lambda i: (i, 0)),
            pl.BlockSpec((tt, LANES), lambda i: (i, 0)),
            pl.BlockSpec((1, D_MODEL), lambda i: (0, 0)),
            pl.BlockSpec((1, D_MODEL), lambda i: (0, 0)),
            pl.BlockSpec(memory_space=pl.ANY),
        ],
        out_specs=[pl.BlockSpec((tt, D_MODEL), lambda i: (i, 0)),
                   pl.BlockSpec((tt, D_MODEL), lambda i: (i, 0))],
        out_shape=[jax.ShapeDtypeStruct((rows, D_MODEL), jnp.float32),
                   jax.ShapeDtypeStruct((rows, D_MODEL), jnp.bfloat16)],
        scratch_shapes=[pltpu.VMEM((2, TOP_K, tt, D_MODEL), jnp.float32),
                        pltpu.SemaphoreType.DMA((2,))],
        compiler_params=_params(1),
        name="combine_ln",
    )(pos3, pos3, x1, route, g2, b2, ysort)


def kernel(x_prompt, x_sample, cache_k, cache_v, page_table, w_in, b_in, lam_q1, lam_k1, lam_q2, lam_k2, subln_g, sg_ln_g, sg_ln_b, w_sp, b_sp, w_pa, w_pb, w_o, ln1_g, ln1_b, w_rg, b_rg, w_re, b_re, w_gate, w_up, w_down, ln2_g, ln2_b):
    batch, seq, _ = x_prompt.shape
    dec_batch, dec_seq, _ = x_sample.shape
    depth = w_in.shape[0]
    n_prompt = batch * seq
    n_sample = dec_batch * dec_seq
    assert seq % ATT_BLOCK == 0 and n_prompt % PROJ_ROW_TILE == 0 and seq % CHUNK == 0
    assert n_sample <= CHUNK and n_sample % 16 == 0 and page_table.shape[1] % PAGES_PER_STEP == 0
    assert dec_seq & (dec_seq - 1) == 0 and dec_seq <= NEW_TOKEN_ROWS
    alpha = (2 * depth) ** 0.25
    bf16 = jnp.bfloat16
    f32 = jnp.float32
    te = EXPERT_TILE
    n_assign = (n_prompt + n_sample) * TOP_K
    n_tiles = -(-n_assign // te) + N_EXPERTS

    xp = x_prompt.reshape(n_prompt, D_MODEL)
    xs = x_sample.reshape(n_sample, D_MODEL)
    xpb = xp.astype(bf16)
    xsb = xs.astype(bf16)
    k_stack = jnp.zeros((depth, n_prompt, N_HEADS, KEY_DIM), f32)
    v_stack = jnp.zeros((depth, n_prompt, N_HEADS, VAL_DIM), f32)
    xsort = jnp.zeros((n_tiles * te, HALF_D), jnp.uint32)

    ridx = jnp.arange(n_sample)
    rt = ridx % dec_seq
    rs = ridx // dec_seq
    smask = ((rt[None, :] <= rt[:, None]) & (rs[None, :] == rs[:, None])).astype(f32)
    tril = jnp.tril(jnp.ones((CHUNK, CHUNK), f32))
    eye_m = jnp.eye(2, dtype=f32)

    ks_rows, vs_rows, sgv_rows = [], [], []
    for l in range(depth):
        lam_i = _lambda_init(l)
        w_in_b = w_in[l].astype(bf16)
        b_row = b_in[l].reshape(1, -1)
        sg_g = sg_ln_g[l].reshape(1, -1)
        sg_b = sg_ln_b[l].reshape(1, -1)
        sg_off = 3 * Q_W
        gate_off = sg_off + 2 * SG_WIDTH
        qkv_p, k_stack, v_stack = _proj_qkv_rows(xpb, w_in_b, b_row, k_stack, v_stack, l)
        qkv_s = _proj(_proj_qkv_kernel, xsb, w_in_b, b_row, 0, 3 * Q_W, f32, name="proj_qkv_s")
        sg_p = _proj(_proj_sg_kernel, xpb, w_in_b, b_row, sg_off, 2 * SG_WIDTH, f32, (sg_g, sg_b), "proj_sg")
        sg_s = _proj(_proj_sg_kernel, xsb, w_in_b, b_row, sg_off, 2 * SG_WIDTH, f32, (sg_g, sg_b), "proj_sg_s")
        gate_p = _proj(_proj_gate_kernel, xpb, w_in_b, b_row, gate_off, 2 * D_MODEL, bf16, name="proj_gate")
        gate_s = _proj(_proj_gate_kernel, xsb, w_in_b, b_row, gate_off, 2 * D_MODEL, bf16, name="proj_gate_s")

        lamp = jnp.stack([lam_q1[l], lam_k1[l], lam_q2[l], lam_k2[l]]).astype(f32)
        ao_p = _flash(qkv_p, lamp, subln_g[l].reshape(VAL_DIM, 1), lam_i, batch, seq)

        q5 = qkv_s[:, :Q_W].reshape(dec_batch, dec_seq, N_HEADS, 2, HEAD_DIM)
        qbig = jnp.einsum("bthmd,mn->bmthnd", q5, eye_m).reshape(
            dec_batch, 2 * dec_seq * N_HEADS, KEY_DIM).astype(bf16)
        k_new = qkv_s[:, Q_W:2 * Q_W].reshape(dec_batch, dec_seq, N_HEADS, KEY_DIM)
        v_new = qkv_s[:, 2 * Q_W:].reshape(dec_batch, dec_seq, N_HEADS, VAL_DIM)
        pad_new = ((0, 0), (0, NEW_TOKEN_ROWS - dec_seq), (0, 0), (0, 0))
        ao_s = _decode(page_table, qbig, cache_k, cache_v, l, jnp.pad(k_new, pad_new), jnp.pad(v_new, pad_new),
                       lamp, subln_g[l].reshape(1, -1), lam_i, dec_seq)
        ao_s = ao_s.reshape(n_sample, N_HEADS * VAL_DIM).astype(bf16)

        wmix_p = (w_sp[l] * tril).astype(bf16)
        wmix_s = (w_sp[l][:, rt[:, None], rt[None, :]] * smask).astype(bf16)
        bmix_p = jnp.repeat(b_sp[l].T, SG_CH, axis=1)
        bmix_s = bmix_p[rt]
        wr = jnp.concatenate([w_rg[l], w_re[l],
                              jnp.zeros((D_MODEL, LANES - N_EGROUPS - N_EXPERTS), f32)], axis=1).astype(bf16)
        br = jnp.concatenate([b_rg[l], b_re[l], jnp.zeros((LANES - N_EGROUPS - N_EXPERTS,), f32)]).reshape(1, -1)
        wts = (w_pa[l].astype(bf16), w_pb[l].astype(bf16), w_o[l].astype(bf16),
               ln1_g[l].reshape(1, -1), ln1_b[l].reshape(1, -1), wr, br)
        x1_p, xpk_p, route_p = _merge(ao_p, sg_p, gate_p, xp, wmix_p, bmix_p, *wts, alpha)
        x1_s, xpk_s, route_s = _merge(ao_s, sg_s, gate_s, xs, wmix_s, bmix_s, *wts, alpha)

        pos, tile_expert, n_active = _moe_plan(route_p, route_s, n_tiles)
        pos_p = pos[:n_prompt * TOP_K].reshape(n_prompt // ROW_TILE, 1, ROW_TILE * TOP_K)
        pos_s = pos[n_prompt * TOP_K:].reshape(1, 1, n_sample * TOP_K)
        xsort = _dispatch(pos_p, xpk_p, xsort)
        xsort = _dispatch(pos_s, xpk_s, xsort)
        ysort = _gmm(tile_expert, n_active, xsort, w_gate, w_up, w_down, l)
        g2 = ln2_g[l].reshape(1, -1)
        b2 = ln2_b[l].reshape(1, -1)
        xp, xpb = _combine(pos_p, x1_p, route_p, g2, b2, ysort, alpha)
        xs, xsb = _combine(pos_s, x1_s, route_s, g2, b2, ysort, alpha)

        ks_rows.append(k_new)
        vs_rows.append(v_new)
        sgv_rows.append(sg_s[:, SG_WIDTH:].reshape(dec_batch, dec_seq, SG_WIDTH))

    yp = xp.reshape(batch, seq, D_MODEL)
    ys = xs.reshape(dec_batch, dec_seq, D_MODEL)
    kv_shape = (depth, batch, seq, N_HEADS, KEY_DIM)
    return (yp, ys, k_stack.reshape(kv_shape), v_stack.reshape(kv_shape), jnp.stack(ks_rows),
            jnp.stack(vs_rows), jnp.stack(sgv_rows))
```

```python
import functools
import math

import jax
import jax.numpy as jnp
from jax import lax
from jax.experimental import pallas as pl
from jax.experimental.pallas import tpu as pltpu

D_MODEL = 2048
N_HEADS = 8
HEAD_DIM = 64
KEY_DIM = 2 * HEAD_DIM
VAL_DIM = 2 * HEAD_DIM
Q_W = N_HEADS * KEY_DIM
SG_WIDTH = 1024
SG_GROUPS = 8
SG_CH = SG_WIDTH // SG_GROUPS
CHUNK = 128
N_EGROUPS = 4
EXP_PER_GROUP = 8
N_EXPERTS = N_EGROUPS * EXP_PER_GROUP
TOP_K = 2
D_EXPERT = 512
LN_EPS = 1e-5
NEG = -1e30
HEAD_SHIFT = N_HEADS.bit_length() - 1
assert 1 << HEAD_SHIFT == N_HEADS
HALF_D = D_MODEL // 2
SCORE_SCALE = HEAD_DIM ** -0.5 * math.log2(math.e)
ONES_ROWS = 16

LANES = 128
COL_BLOCK = 1024
ROW_TILE = 256
PROJ_ROW_TILE = 1024
ATT_BLOCK = 512
ATT_COLS = 512
PAGES_PER_STEP = 8
NEW_TOKEN_ROWS = 16
EXPERT_TILE = 256
VMEM_LIMIT = 56 * 1024 * 1024


def _lambda_init(layer_idx):
    return 0.8 - 0.6 * math.exp(-0.3 * layer_idx)


def _layer_norm_rows(x, g, b):
    mu = jnp.mean(x, axis=-1, keepdims=True)
    xc = x - mu
    var = jnp.mean(xc * xc, axis=-1, keepdims=True)
    return xc * lax.rsqrt(var + LN_EPS) * g + b


def _lam_value(lamp_ref, lam_i):
    a = jnp.sum(lamp_ref[0:1, :] * lamp_ref[1:2, :], axis=-1, keepdims=True)
    b = jnp.sum(lamp_ref[2:3, :] * lamp_ref[3:4, :], axis=-1, keepdims=True)
    return jnp.exp(a) - jnp.exp(b) + lam_i


def _params(n_axes):
    return pltpu.CompilerParams(dimension_semantics=("arbitrary",) * n_axes, vmem_limit_bytes=VMEM_LIMIT)


def _qkv_block(x_ref, w_ref, b_ref):
    z = jnp.dot(x_ref[...], w_ref[...], preferred_element_type=jnp.float32) + b_ref[...]
    scale = jnp.where(pl.program_id(0) == 0, SCORE_SCALE, 1.0).astype(jnp.float32)
    return z * scale


def _proj_qkv_kernel(x_ref, w_ref, b_ref, o_ref):
    o_ref[...] = _qkv_block(x_ref, w_ref, b_ref)


def _proj_qkv_rows_kernel(x_ref, w_ref, b_ref, kin_ref, vin_ref, o_ref, ko_ref, vo_ref):
    del kin_ref, vin_ref
    z = _qkv_block(x_ref, w_ref, b_ref)
    o_ref[...] = z
    j = pl.program_id(0)

    def per_head(dst_ref):
        for h in range(N_HEADS):
            dst_ref[:, h, :] = z[:, h * KEY_DIM:(h + 1) * KEY_DIM]

    @pl.when(j == 1)
    def _():
        per_head(ko_ref)

    @pl.when(j == 2)
    def _():
        per_head(vo_ref)


def _proj_sg_kernel(x_ref, w_ref, b_ref, g_ref, bb_ref, o_ref):
    z = jnp.dot(x_ref[...], w_ref[...], preferred_element_type=jnp.float32) + b_ref[...]
    y = jax.nn.gelu(z)
    j = pl.program_id(0)

    @pl.when(j == 0)
    def _():
        o_ref[...] = y

    @pl.when(j == 1)
    def _():
        o_ref[...] = _layer_norm_rows(y, g_ref[...], bb_ref[...])


def _proj_gate_kernel(x_ref, w_ref, b_ref, o_ref):
    z = jnp.dot(x_ref[...], w_ref[...], preferred_element_type=jnp.float32) + b_ref[...]
    o_ref[...] = (1.0 / (1.0 + jnp.exp(-z))).astype(o_ref.dtype)


def _proj_in_specs(tm, off):
    return [
        pl.BlockSpec((tm, D_MODEL), lambda j, i: (i, 0)),
        pl.BlockSpec((D_MODEL, COL_BLOCK), lambda j, i: (0, off + j)),
        pl.BlockSpec((1, COL_BLOCK), lambda j, i: (0, off + j)),
    ]


def _proj(body, xb, w, b, col_off, n_col, out_dtype, extra=(), name=None):
    rows = xb.shape[0]
    tm = min(PROJ_ROW_TILE, rows)
    extra_specs = [pl.BlockSpec((1, COL_BLOCK), lambda j, i: (0, 0)) for _ in extra]
    return pl.pallas_call(
        body,
        grid=(n_col // COL_BLOCK, rows // tm),
        in_specs=_proj_in_specs(tm, col_off // COL_BLOCK) + extra_specs,
        out_specs=pl.BlockSpec((tm, COL_BLOCK), lambda j, i: (i, j)),
        out_shape=jax.ShapeDtypeStruct((rows, n_col), out_dtype),
        compiler_params=_params(2),
        name=name,
    )(xb, w, b, *extra)


def _proj_qkv_rows(xb, w, b, k_stack, v_stack, layer):
    rows = xb.shape[0]
    tm = PROJ_ROW_TILE
    n_i = rows // tm

    def k_rows(j, i):
        return (layer, jnp.where(j < 1, 0, jnp.where(j == 1, i, n_i - 1)), 0, 0)

    def v_rows(j, i):
        return (layer, jnp.where(j < 2, 0, i), 0, 0)

    any_spec = pl.BlockSpec(memory_space=pl.ANY)
    return pl.pallas_call(
        _proj_qkv_rows_kernel,
        grid=(3, n_i),
        in_specs=_proj_in_specs(tm, 0) + [any_spec, any_spec],
        out_specs=[pl.BlockSpec((tm, COL_BLOCK), lambda j, i: (i, j)),
                   pl.BlockSpec((None, tm, N_HEADS, KEY_DIM), k_rows),
                   pl.BlockSpec((None, tm, N_HEADS, VAL_DIM), v_rows)],
        out_shape=[jax.ShapeDtypeStruct((rows, 3 * Q_W), jnp.float32),
                   jax.ShapeDtypeStruct(k_stack.shape, k_stack.dtype),
                   jax.ShapeDtypeStruct(v_stack.shape, v_stack.dtype)],
        input_output_aliases={3: 1, 4: 2},
        compiler_params=_params(2),
        name="proj_qkv_rows",
    )(xb, w, b, k_stack, v_stack)


def _flash_kernel(q_ref, k_ref, v_ref, lamp_ref, g_ref, o_ref,
                  kb_ref, vt_ref, qs_ref, m_ref, acc_ref, *, lam_i, n_blk):
    tq = ATT_BLOCK
    qi = pl.program_id(1)

    @pl.when(qi == 0)
    def _():
        for c in range(n_blk):
            rows = slice(c * tq, (c + 1) * tq)
            kb_ref[c] = k_ref[rows, :].astype(jnp.bfloat16)
            vt_ref[c, :VAL_DIM, :] = v_ref[rows, :].T.astype(jnp.bfloat16)
            vt_ref[c, VAL_DIM:, :] = jnp.ones((ONES_ROWS, tq), jnp.bfloat16)

    qt = q_ref[...].T
    row = lax.broadcasted_iota(jnp.int32, qt.shape, 0)
    qs_ref[:, :tq] = jnp.where(row < HEAD_DIM, qt, 0.0).astype(jnp.bfloat16)
    qs_ref[:, tq:] = jnp.where(row >= HEAD_DIM, qt, 0.0).astype(jnp.bfloat16)
    m_ref[...] = jnp.full(m_ref.shape, NEG, jnp.float32)
    acc_ref[...] = jnp.zeros(acc_ref.shape, jnp.float32)

    def step(ki, masked):
        kblk = kb_ref[ki]
        vblk = vt_ref[ki]
        n_chains = 2 * tq // ATT_COLS

        def scores(c):
            cols = slice(c * ATT_COLS, (c + 1) * ATT_COLS)
            s = jnp.dot(kblk, qs_ref[:, cols], preferred_element_type=jnp.float32)
            if masked:
                kpos = lax.broadcasted_iota(jnp.int32, s.shape, 0)
                qpos = lax.broadcasted_iota(jnp.int32, s.shape, 1) + (c * ATT_COLS) % tq
                s = jnp.where(kpos <= qpos, s, NEG)
            return s

        new_state = []
        s_next = scores(0)
        for c in range(n_chains):
            cols = slice(c * ATT_COLS, (c + 1) * ATT_COLS)
            s = s_next
            if c + 1 < n_chains:
                s_next = scores(c + 1)
            m_prev = m_ref[:, cols]
            m_cur = jnp.maximum(m_prev, jnp.max(s, axis=0, keepdims=True))
            alpha = jnp.exp2(m_prev - m_cur)
            p = jnp.exp2(s - m_cur)
            acc_new = alpha * acc_ref[:, cols] + jnp.dot(
                vblk, p.astype(jnp.bfloat16), preferred_element_type=jnp.float32)
            new_state.append((cols, m_cur, acc_new))
        for cols, m_cur, acc_new in new_state:
            m_ref[:, cols] = m_cur
            acc_ref[:, cols] = acc_new

    def body(ki, carry):
        step(ki, False)
        return carry

    lax.fori_loop(0, qi, body, 0)
    step(qi, True)

    lam = _lam_value(lamp_ref, lam_i)
    l_row = acc_ref[VAL_DIM:VAL_DIM + 1, :]
    o = (acc_ref[:VAL_DIM, :tq] / l_row[:, :tq]) - lam * (acc_ref[:VAL_DIM, tq:] / l_row[:, tq:])
    ms = jnp.mean(o * o, axis=0, keepdims=True)
    r = o * lax.rsqrt(ms + LN_EPS) * g_ref[...] * (1.0 - lam_i)
    o_ref[...] = r.T.astype(o_ref.dtype)


def _flash(qkv, lamp, g_col, lam_i, batch, seq):
    tq = ATT_BLOCK
    n_blk = seq // tq
    kcol = Q_W // KEY_DIM
    vcol = 2 * Q_W // KEY_DIM
    return pl.pallas_call(
        functools.partial(_flash_kernel, lam_i=lam_i, n_blk=n_blk),
        grid=(batch * N_HEADS, n_blk),
        in_specs=[
            pl.BlockSpec((tq, KEY_DIM), lambda bh, qi: ((bh // N_HEADS) * n_blk + qi, bh % N_HEADS)),
            pl.BlockSpec((seq, KEY_DIM), lambda bh, qi: (bh // N_HEADS, kcol + bh % N_HEADS)),
            pl.BlockSpec((seq, VAL_DIM), lambda bh, qi: (bh // N_HEADS, vcol + bh % N_HEADS)),
            pl.BlockSpec((4, HEAD_DIM), lambda bh, qi: (0, 0)),
            pl.BlockSpec((VAL_DIM, 1), lambda bh, qi: (0, 0)),
        ],
        out_specs=pl.BlockSpec((tq, VAL_DIM), lambda bh, qi: ((bh // N_HEADS) * n_blk + qi, bh % N_HEADS)),
        out_shape=jax.ShapeDtypeStruct((batch * seq, N_HEADS * VAL_DIM), jnp.bfloat16),
        scratch_shapes=[
            pltpu.VMEM((n_blk, tq, KEY_DIM), jnp.bfloat16),
            pltpu.VMEM((n_blk, VAL_DIM + ONES_ROWS, tq), jnp.bfloat16),
            pltpu.VMEM((KEY_DIM, 2 * tq), jnp.bfloat16),
            pltpu.VMEM((1, 2 * tq), jnp.float32),
            pltpu.VMEM((VAL_DIM + ONES_ROWS, 2 * tq), jnp.float32),
        ],
        compiler_params=_params(2),
        name="flash_diff_attn",
    )(qkv, qkv, qkv, lamp, g_col)


def _decode_kernel(pt_ref, qbig_ref, *refs, lam_i, dec_seq):
    del pt_ref
    pp = PAGES_PER_STEP
    k_refs = refs[:pp]
    v_refs = refs[pp:2 * pp]
    kn_ref, vn_ref, lamp_ref, g_ref, o_ref, m_ref, l_ref, acc_ref = refs[2 * pp:]
    p_id = pl.program_id(1)
    nq = qbig_ref.shape[0]
    half = nq // 2

    @pl.when(p_id == 0)
    def _():
        m_ref[...] = jnp.full(m_ref.shape, NEG, jnp.float32)
        l_ref[...] = jnp.zeros(l_ref.shape, jnp.float32)
        acc_ref[...] = jnp.zeros(acc_ref.shape, jnp.float32)

    def flat(refs_, width):
        parts = [r[...].reshape(r.shape[0] * N_HEADS, width) for r in refs_]
        rows = parts[0] if len(parts) == 1 else jnp.concatenate(parts, axis=0)
        return rows.astype(jnp.bfloat16)

    def update(k2, v2, causal):
        s = lax.dot_general(qbig_ref[...], k2, (((1,), (1,)), ((), ())),
                            preferred_element_type=jnp.float32)
        qrow = lax.broadcasted_iota(jnp.int32, s.shape, 0)
        kcol = lax.broadcasted_iota(jnp.int32, s.shape, 1)
        ok = (qrow & (N_HEADS - 1)) == (kcol & (N_HEADS - 1))
        if causal:
            q_t = (qrow >> HEAD_SHIFT) & (dec_seq - 1)
            ok = ok & ((kcol >> HEAD_SHIFT) <= q_t)
        s = jnp.where(ok, s, NEG)
        m_prev = m_ref[...]
        m_cur = jnp.maximum(m_prev, jnp.max(s, axis=-1, keepdims=True))
        alpha = jnp.exp2(m_prev - m_cur)
        p = jnp.exp2(s - m_cur)
        l_ref[...] = alpha * l_ref[...] + jnp.sum(p, axis=-1, keepdims=True)
        acc_ref[...] = alpha * acc_ref[...] + jnp.dot(
            p.astype(jnp.bfloat16), v2, preferred_element_type=jnp.float32)
        m_ref[...] = m_cur

    update(flat(k_refs, KEY_DIM), flat(v_refs, VAL_DIM), False)

    @pl.when(p_id == pl.num_programs(1) - 1)
    def _():
        update(flat([kn_ref], KEY_DIM), flat([vn_ref], VAL_DIM), True)
        lam = _lam_value(lamp_ref, lam_i)
        o1 = acc_ref[0:half, :] / l_ref[0:half, :]
        o2 = acc_ref[half:nq, :] / l_ref[half:nq, :]
        d = o1 - lam * o2
        ms = jnp.mean(d * d, axis=-1, keepdims=True)
        o_ref[...] = d * lax.rsqrt(ms + LN_EPS) * g_ref[...] * (1.0 - lam_i)


def _decode(page_table, qbig, cache_k, cache_v, layer, kn, vn, lamp, g_row, lam_i, dec_seq):
    dec_batch, n_pages = page_table.shape
    pp = PAGES_PER_STEP
    page = cache_k.shape[2]
    nq = qbig.shape[1]
    new_rows = kn.shape[1]

    def page_spec(r):
        return pl.BlockSpec((None, None, page, N_HEADS, KEY_DIM),
                            lambda b, p, pt: (layer, pt[b, p * pp + r], 0, 0, 0))

    grid_spec = pltpu.PrefetchScalarGridSpec(
        num_scalar_prefetch=1,
        grid=(dec_batch, n_pages // pp),
        in_specs=[pl.BlockSpec((None, nq, KEY_DIM), lambda b, p, pt: (b, 0, 0))]
        + [page_spec(r) for r in range(pp)] + [page_spec(r) for r in range(pp)]
        + [pl.BlockSpec((None, new_rows, N_HEADS, KEY_DIM), lambda b, p, pt: (b, 0, 0, 0)),
           pl.BlockSpec((None, new_rows, N_HEADS, VAL_DIM), lambda b, p, pt: (b, 0, 0, 0)),
           pl.BlockSpec((4, HEAD_DIM), lambda b, p, pt: (0, 0)),
           pl.BlockSpec((1, VAL_DIM), lambda b, p, pt: (0, 0))],
        out_specs=pl.BlockSpec((None, nq // 2, VAL_DIM), lambda b, p, pt: (b, 0, 0)),
        scratch_shapes=[
            pltpu.VMEM((nq, 1), jnp.float32),
            pltpu.VMEM((nq, 1), jnp.float32),
            pltpu.VMEM((nq, VAL_DIM), jnp.float32),
        ],
    )
    return pl.pallas_call(
        functools.partial(_decode_kernel, lam_i=lam_i, dec_seq=dec_seq),
        grid_spec=grid_spec,
        out_shape=jax.ShapeDtypeStruct((dec_batch, nq // 2, VAL_DIM), jnp.float32),
        compiler_params=_params(2),
        name="paged_diff_attn",
    )(page_table, qbig, *([cache_k] * pp), *([cache_v] * pp), kn, vn, lamp, g_row)


def _merge_kernel(ao_ref, sg_ref, gate_ref, x_ref, wmix_ref, bmix_ref, wpa_ref, wpb_ref, wo_ref,
                  g1_ref, b1_ref, wr_ref, br_ref, x1_ref, xpk_ref, route_ref, so_ref, *, alpha):
    tm = x_ref.shape[0]
    ch = wmix_ref.shape[-1]
    for c in range(tm // ch):
        rows = slice(c * ch, (c + 1) * ch)
        for g in range(SG_GROUPS):
            cols = slice(g * SG_CH, (g + 1) * SG_CH)
            vs = sg_ref[rows, SG_WIDTH + g * SG_CH:SG_WIDTH + (g + 1) * SG_CH].astype(jnp.bfloat16)
            mixed = jnp.dot(wmix_ref[g], vs, preferred_element_type=jnp.float32) + bmix_ref[:, cols]
            so_ref[rows, cols] = (sg_ref[rows, cols] * mixed).astype(jnp.bfloat16)

    a = jnp.dot(ao_ref[...], wpa_ref[...], preferred_element_type=jnp.float32)
    bm = jnp.dot(so_ref[...], wpb_ref[...], preferred_element_type=jnp.float32)
    merged = gate_ref[:, :D_MODEL].astype(jnp.float32) * a + gate_ref[:, D_MODEL:].astype(jnp.float32) * bm
    y = jnp.dot(merged.astype(jnp.bfloat16), wo_ref[...], preferred_element_type=jnp.float32)
    x1 = _layer_norm_rows(alpha * x_ref[...] + y, g1_ref[...], b1_ref[...])
    x1_ref[...] = x1

    x1b = x1.astype(jnp.bfloat16)
    lo = lax.bitcast_convert_type(x1b[:, :HALF_D].astype(jnp.float32), jnp.uint32)
    hi = lax.bitcast_convert_type(x1b[:, HALF_D:].astype(jnp.float32), jnp.uint32)
    xpk_ref[...] = (lo >> 16) | (hi & jnp.uint32(0xFFFF0000))

    logits = jnp.dot(x1b, wr_ref[...], preferred_element_type=jnp.float32) + br_ref[...]
    lane = lax.broadcasted_iota(jnp.int32, logits.shape, 1)
    ninf = -jnp.inf
    gmask = lane < N_EGROUPS
    gl = jnp.where(gmask, logits, ninf)
    gmax = jnp.max(gl, axis=-1, keepdims=True)
    gi = jnp.min(jnp.where(gl == gmax, lane, LANES), axis=-1, keepdims=True)
    gsum = jnp.sum(jnp.where(gmask, jnp.exp(gl - gmax), 0.0), axis=-1, keepdims=True)
    gw = 1.0 / gsum
    emask = (lane >= N_EGROUPS) & (lane < N_EGROUPS + N_EXPERTS) & (((lane - N_EGROUPS) // EXP_PER_GROUP) == gi)
    el = jnp.where(emask, logits, ninf)
    v1 = jnp.max(el, axis=-1, keepdims=True)
    i1 = jnp.min(jnp.where(el == v1, lane, LANES), axis=-1, keepdims=True)
    el2 = jnp.where(lane == i1, ninf, el)
    v2 = jnp.max(el2, axis=-1, keepdims=True)
    i2 = jnp.min(jnp.where(el2 == v2, lane, LANES), axis=-1, keepdims=True)
    e2 = jnp.exp(v2 - v1)
    den = 1.0 + e2
    tw1 = (1.0 / den) * gw
    tw2 = (e2 / den) * gw
    id1 = (i1 - N_EGROUPS).astype(jnp.float32)
    id2 = (i2 - N_EGROUPS).astype(jnp.float32)
    route_ref[...] = jnp.where(lane == 0, id1, jnp.where(lane == 1, id2,
                               jnp.where(lane == 2, tw1, jnp.where(lane == 3, tw2, 0.0))))


def _merge(ao, sg, gate, x, wmix, bmix, wpa, wpb, wo, g1, b1, wr, br, alpha):
    rows = x.shape[0]
    tm = min(ROW_TILE, rows)

    def full(arr):
        nd = arr.ndim
        return pl.BlockSpec(arr.shape, lambda i, _nd=nd: (0,) * _nd, pipeline_mode=pl.Buffered(1))

    return pl.pallas_call(
        functools.partial(_merge_kernel, alpha=alpha),
        grid=(rows // tm,),
        in_specs=[
            pl.BlockSpec((tm, ao.shape[1]), lambda i: (i, 0)),
            pl.BlockSpec((tm, sg.shape[1]), lambda i: (i, 0)),
            pl.BlockSpec((tm, gate.shape[1]), lambda i: (i, 0)),
            pl.BlockSpec((tm, D_MODEL), lambda i: (i, 0)),
            full(wmix), full(bmix), full(wpa), full(wpb), full(wo), full(g1), full(b1), full(wr), full(br),
        ],
        out_specs=[pl.BlockSpec((tm, D_MODEL), lambda i: (i, 0)),
                   pl.BlockSpec((tm, HALF_D), lambda i: (i, 0)),
                   pl.BlockSpec((tm, LANES), lambda i: (i, 0))],
        out_shape=[jax.ShapeDtypeStruct((rows, D_MODEL), jnp.float32),
                   jax.ShapeDtypeStruct((rows, HALF_D), jnp.uint32),
                   jax.ShapeDtypeStruct((rows, LANES), jnp.float32)],
        scratch_shapes=[pltpu.VMEM((tm, SG_WIDTH), jnp.bfloat16)],
        compiler_params=_params(1),
        name="merge_ln_router",
    )(ao, sg, gate, x, wmix, bmix, wpa, wpb, wo, g1, b1, wr, br)


def _moe_plan(route_p, route_s, n_tiles):
    te = EXPERT_TILE
    ef = jnp.concatenate([route_p[:, :TOP_K], route_s[:, :TOP_K]], axis=0).astype(jnp.int32).reshape(-1)
    onehot = (ef[:, None] == jnp.arange(N_EXPERTS, dtype=jnp.int32)[None, :]).astype(jnp.int32)
    csum = jnp.cumsum(onehot, axis=0)
    counts = csum[-1]
    rank = jnp.sum(csum * onehot, axis=1) - 1
    pcounts = ((counts + te - 1) // te) * te
    pend = jnp.cumsum(pcounts)
    pstart = pend - pcounts
    pos = jnp.sum(onehot * pstart[None, :], axis=1) + rank
    n_active = (pend[-1] // te).astype(jnp.int32).reshape(1)
    tile_start = jnp.minimum(jnp.arange(n_tiles, dtype=jnp.int32) * te, pend[-1] - 1)
    tile_expert = jnp.minimum(jnp.sum((tile_start[:, None] >= pend[None, :]).astype(jnp.int32), axis=1),
                              N_EXPERTS - 1).astype(jnp.int32)
    eids = jnp.arange(N_EXPERTS, dtype=jnp.int32)[None, :]
    present = (counts > 0)[None, :]
    slot = jnp.sum((present & (eids < tile_expert[:, None])).astype(jnp.int32), axis=1) % 2
    later = jnp.min(jnp.where(present & (eids > tile_expert[:, None]), eids, N_EXPERTS), axis=1)
    next_expert = jnp.where(later < N_EXPERTS, later, -1).astype(jnp.int32)
    return pos.astype(jnp.int32), tile_expert, n_active, slot.astype(jnp.int32), next_expert


def _dispatch_kernel(pos_ref, x_ref, xs_in, xs_hbm, xbuf, sems):
    del xs_in
    tt = x_ref.shape[0]
    i = pl.program_id(0)
    slot = i % 2

    def wait_tile(s):
        for _ in range(TOP_K):
            pltpu.make_async_copy(xbuf.at[s], xs_hbm.at[pl.ds(0, tt)], sems.at[s]).wait()

    @pl.when(i >= 2)
    def _():
        wait_tile(slot)

    xbuf[slot] = x_ref[...]

    def body(r, c):
        for k in range(TOP_K):
            pltpu.make_async_copy(xbuf.at[slot, pl.ds(r, 1)],
                                  xs_hbm.at[pl.ds(pos_ref[0, 0, r * TOP_K + k], 1)], sems.at[slot]).start()
        return c

    lax.fori_loop(0, tt, body, 0, unroll=8)

    n = pl.num_programs(0)

    @pl.when(i == n - 1)
    def _():
        wait_tile(slot)

        @pl.when(n >= 2)
        def _():
            wait_tile(1 - slot)


def _dispatch(pos3, xpk, xsort):
    n_tok_tiles = pos3.shape[0]
    tt = pos3.shape[2] // TOP_K
    return pl.pallas_call(
        _dispatch_kernel,
        grid=(n_tok_tiles,),
        in_specs=[pl.BlockSpec((1, 1, pos3.shape[2]), lambda i: (i, 0, 0), memory_space=pltpu.SMEM),
                  pl.BlockSpec((tt, HALF_D), lambda i: (i, 0)),
                  pl.BlockSpec(memory_space=pl.ANY)],
        out_specs=pl.BlockSpec(memory_space=pl.ANY),
        out_shape=jax.ShapeDtypeStruct(xsort.shape, xsort.dtype),
        scratch_shapes=[pltpu.VMEM((2, tt, HALF_D), jnp.uint32), pltpu.SemaphoreType.DMA((2,))],
        input_output_aliases={2: 0},
        compiler_params=_params(1),
        name="moe_dispatch",
    )(pos3, xpk, xsort)


def _gmm_kernel(te_ref, na_ref, slot_ref, next_ref, xs_ref, wg_hbm, wu_hbm, wd_hbm, y_ref,
                wgf_ref, wuf_ref, wdf_ref, wgb_ref, wub_ref, wdb_ref, sems, *, layer):
    t = pl.program_id(0)
    n_active = na_ref[0]

    def fetch(e, s):
        return (pltpu.make_async_copy(wg_hbm.at[layer, e], wgf_ref.at[s], sems.at[s, 0]),
                pltpu.make_async_copy(wu_hbm.at[layer, e], wuf_ref.at[s], sems.at[s, 1]),
                pltpu.make_async_copy(wd_hbm.at[layer, e], wdf_ref.at[s], sems.at[s, 2]))

    @pl.when((t == 0) & (n_active > 0))
    def _():
        for c in fetch(te_ref[0], 0):
            c.start()

    @pl.when(t < n_active)
    def _():
        prev = te_ref[jnp.maximum(t - 1, 0)]

        @pl.when((t == 0) | (te_ref[t] != prev))
        def _():
            s = slot_ref[t]
            for c in fetch(te_ref[t], s):
                c.wait()
            wgb_ref[...] = wgf_ref[s].astype(jnp.bfloat16)
            wub_ref[...] = wuf_ref[s].astype(jnp.bfloat16)
            wdb_ref[...] = wdf_ref[s].astype(jnp.bfloat16)
            nxt = next_ref[t]

            @pl.when(nxt >= 0)
            def _():
                for c in fetch(nxt, 1 - s):
                    c.start()

        u = xs_ref[...]
        lo = lax.bitcast_convert_type(u << 16, jnp.float32).astype(jnp.bfloat16)
        hi = lax.bitcast_convert_type(u & jnp.uint32(0xFFFF0000), jnp.float32).astype(jnp.bfloat16)

        def up(w_ref):
            return (jnp.dot(lo, w_ref[:HALF_D, :], preferred_element_type=jnp.float32)
                    + jnp.dot(hi, w_ref[HALF_D:, :], preferred_element_type=jnp.float32))

        hg = up(wgb_ref)
        hu = up(wub_ref)
        act = (hg * (1.0 / (1.0 + jnp.exp(-hg))) * hu).astype(jnp.bfloat16)
        y_ref[...] = jnp.dot(act, wdb_ref[...], preferred_element_type=jnp.float32)

    @pl.when(t >= n_active)
    def _():
        y_ref[...] = jnp.zeros(y_ref.shape, jnp.float32)


def _gmm(tile_expert, n_active, slot, next_expert, xsort, w_gate, w_up, w_down, layer):
    te = EXPERT_TILE
    n_tiles = xsort.shape[0] // te
    any_spec = pl.BlockSpec(memory_space=pl.ANY)
    grid_spec = pltpu.PrefetchScalarGridSpec(
        num_scalar_prefetch=4,
        grid=(n_tiles,),
        in_specs=[pl.BlockSpec((te, HALF_D), lambda t, *_: (t, 0)), any_spec, any_spec, any_spec],
        out_specs=pl.BlockSpec((te, D_MODEL), lambda t, *_: (t, 0)),
        scratch_shapes=[
            pltpu.VMEM((2, D_MODEL, D_EXPERT), jnp.float32),
            pltpu.VMEM((2, D_MODEL, D_EXPERT), jnp.float32),
            pltpu.VMEM((2, D_EXPERT, D_MODEL), jnp.float32),
            pltpu.VMEM((D_MODEL, D_EXPERT), jnp.bfloat16),
            pltpu.VMEM((D_MODEL, D_EXPERT), jnp.bfloat16),
            pltpu.VMEM((D_EXPERT, D_MODEL), jnp.bfloat16),
            pltpu.SemaphoreType.DMA((2, 3)),
        ],
    )
    return pl.pallas_call(
        functools.partial(_gmm_kernel, layer=layer),
        grid_spec=grid_spec,
        out_shape=jax.ShapeDtypeStruct((xsort.shape[0], D_MODEL), jnp.float32),
        compiler_params=_params(1),
        name="moe_grouped",
    )(tile_expert, n_active, slot, next_expert, xsort, w_gate, w_up, w_down)


def _combine_kernel(pos_ref, posn_ref, x1_ref, route_ref, g_ref, b_ref, y_hbm, x2_ref, x2b_ref,
                    gbuf, sem, *, alpha):
    tt = x1_ref.shape[0]
    i = pl.program_id(0)
    slot = i % 2

    def start(idx_ref, s_):
        def body(r, c):
            for k in range(TOP_K):
                pltpu.make_async_copy(y_hbm.at[pl.ds(idx_ref[0, 0, r * TOP_K + k], 1)],
                                      gbuf.at[s_, k, pl.ds(r, 1)], sem.at[s_]).start()
            return c
        lax.fori_loop(0, tt, body, 0, unroll=8)

    @pl.when(i == 0)
    def _():
        start(pos_ref, 0)

    for k in range(TOP_K):
        pltpu.make_async_copy(y_hbm.at[pl.ds(0, tt)], gbuf.at[slot, k], sem.at[slot]).wait()

    @pl.when(i + 1 < pl.num_programs(0))
    def _():
        start(posn_ref, 1 - slot)

    ff = (route_ref[:, TOP_K:TOP_K + 1] * gbuf[slot, 0]
          + route_ref[:, TOP_K + 1:TOP_K + 2] * gbuf[slot, 1])
    x2 = _layer_norm_rows(alpha * x1_ref[...] + ff, g_ref[...], b_ref[...])
    x2_ref[...] = x2
    x2b_ref[...] = x2.astype(jnp.bfloat16)


def _combine(pos3, x1, route, g2, b2, ysort, alpha):
    rows = x1.shape[0]
    tt = pos3.shape[2] // TOP_K
    n_i = rows // tt
    return pl.pallas_call(
        functools.partial(_combine_kernel, alpha=alpha),
        grid=(n_i,),
        in_specs=[
            pl.BlockSpec((1, 1, tt * TOP_K), lambda i: (i, 0, 0), memory_space=pltpu.SMEM),
            pl.BlockSpec((1, 1, tt * TOP_K), lambda i: (jnp.minimum(i + 1, n_i - 1), 0, 0),
                         memory_space=pltpu.SMEM),
            pl.BlockSpec((tt, D_MODEL), lambda i: (i, 0)),
            pl.BlockSpec((tt, LANES), lambda i: (i, 0)),
            pl.BlockSpec((1, D_MODEL), lambda i: (0, 0)),
            pl.BlockSpec((1, D_MODEL), lambda i: (0, 0)),
            pl.BlockSpec(memory_space=pl.ANY),
        ],
        out_specs=[pl.BlockSpec((tt, D_MODEL), lambda i: (i, 0)),
                   pl.BlockSpec((tt, D_MODEL), lambda i: (i, 0))],
        out_shape=[jax.ShapeDtypeStruct((rows, D_MODEL), jnp.float32),
                   jax.ShapeDtypeStruct((rows, D_MODEL), jnp.bfloat16)],
        scratch_shapes=[pltpu.VMEM((2, TOP_K, tt, D_MODEL), jnp.float32),
                        pltpu.SemaphoreType.DMA((2,))],
        compiler_params=_params(1),
        name="combine_ln",
    )(pos3, pos3, x1, route, g2, b2, ysort)


def kernel(x_prompt, x_sample, cache_k, cache_v, page_table, w_in, b_in, lam_q1, lam_k1, lam_q2, lam_k2, subln_g, sg_ln_g, sg_ln_b, w_sp, b_sp, w_pa, w_pb, w_o, ln1_g, ln1_b, w_rg, b_rg, w_re, b_re, w_gate, w_up, w_down, ln2_g, ln2_b):
    batch, seq, _ = x_prompt.shape
    dec_batch, dec_seq, _ = x_sample.shape
    depth = w_in.shape[0]
    n_prompt = batch * seq
    n_sample = dec_batch * dec_seq
    assert seq % ATT_BLOCK == 0 and n_prompt % PROJ_ROW_TILE == 0 and seq % CHUNK == 0
    assert n_sample <= CHUNK and n_sample % 16 == 0 and page_table.shape[1] % PAGES_PER_STEP == 0
    assert dec_seq & (dec_seq - 1) == 0 and dec_seq <= NEW_TOKEN_ROWS
    alpha = (2 * depth) ** 0.25
    bf16 = jnp.bfloat16
    f32 = jnp.float32
    te = EXPERT_TILE
    n_assign = (n_prompt + n_sample) * TOP_K
    n_tiles = -(-n_assign // te) + N_EXPERTS

    xp = x_prompt.reshape(n_prompt, D_MODEL)
    xs = x_sample.reshape(n_sample, D_MODEL)
    xpb = xp.astype(bf16)
    xsb = xs.astype(bf16)
    k_stack = jnp.zeros((depth, n_prompt, N_HEADS, KEY_DIM), f32)
    v_stack = jnp.zeros((depth, n_prompt, N_HEADS, VAL_DIM), f32)
    xsort = jnp.zeros((n_tiles * te, HALF_D), jnp.uint32)

    ridx = jnp.arange(n_sample)
    rt = ridx % dec_seq
    rs = ridx // dec_seq
    smask = ((rt[None, :] <= rt[:, None]) & (rs[None, :] == rs[:, None])).astype(f32)
    tril = jnp.tril(jnp.ones((CHUNK, CHUNK), f32))
    eye_m = jnp.eye(2, dtype=f32)

    ks_rows, vs_rows, sgv_rows = [], [], []
    for l in range(depth):
        lam_i = _lambda_init(l)
        w_in_b = w_in[l].astype(bf16)
        b_row = b_in[l].reshape(1, -1)
        sg_g = sg_ln_g[l].reshape(1, -1)
        sg_b = sg_ln_b[l].reshape(1, -1)
        sg_off = 3 * Q_W
        gate_off = sg_off + 2 * SG_WIDTH
        qkv_p, k_stack, v_stack = _proj_qkv_rows(xpb, w_in_b, b_row, k_stack, v_stack, l)
        qkv_s = _proj(_proj_qkv_kernel, xsb, w_in_b, b_row, 0, 3 * Q_W, f32, name="proj_qkv_s")
        sg_p = _proj(_proj_sg_kernel, xpb, w_in_b, b_row, sg_off, 2 * SG_WIDTH, f32, (sg_g, sg_b), "proj_sg")
        sg_s = _proj(_proj_sg_kernel, xsb, w_in_b, b_row, sg_off, 2 * SG_WIDTH, f32, (sg_g, sg_b), "proj_sg_s")
        gate_p = _proj(_proj_gate_kernel, xpb, w_in_b, b_row, gate_off, 2 * D_MODEL, bf16, name="proj_gate")
        gate_s = _proj(_proj_gate_kernel, xsb, w_in_b, b_row, gate_off, 2 * D_MODEL, bf16, name="proj_gate_s")

        lamp = jnp.stack([lam_q1[l], lam_k1[l], lam_q2[l], lam_k2[l]]).astype(f32)
        ao_p = _flash(qkv_p, lamp, subln_g[l].reshape(VAL_DIM, 1), lam_i, batch, seq)

        q5 = qkv_s[:, :Q_W].reshape(dec_batch, dec_seq, N_HEADS, 2, HEAD_DIM)
        qbig = jnp.einsum("bthmd,mn->bmthnd", q5, eye_m).reshape(
            dec_batch, 2 * dec_seq * N_HEADS, KEY_DIM).astype(bf16)
        k_new = qkv_s[:, Q_W:2 * Q_W].reshape(dec_batch, dec_seq, N_HEADS, KEY_DIM)
        v_new = qkv_s[:, 2 * Q_W:].reshape(dec_batch, dec_seq, N_HEADS, VAL_DIM)
        pad_new = ((0, 0), (0, NEW_TOKEN_ROWS - dec_seq), (0, 0), (0, 0))
        ao_s = _decode(page_table, qbig, cache_k, cache_v, l, jnp.pad(k_new, pad_new), jnp.pad(v_new, pad_new),
                       lamp, subln_g[l].reshape(1, -1), lam_i, dec_seq)
        ao_s = ao_s.reshape(n_sample, N_HEADS * VAL_DIM).astype(bf16)

        wmix_p = (w_sp[l] * tril).astype(bf16)
        wmix_s = (w_sp[l][:, rt[:, None], rt[None, :]] * smask).astype(bf16)
        bmix_p = jnp.repeat(b_sp[l].T, SG_CH, axis=1)
        bmix_s = bmix_p[rt]
        wr = jnp.concatenate([w_rg[l], w_re[l],
                              jnp.zeros((D_MODEL, LANES - N_EGROUPS - N_EXPERTS), f32)], axis=1).astype(bf16)
        br = jnp.concatenate([b_rg[l], b_re[l], jnp.zeros((LANES - N_EGROUPS - N_EXPERTS,), f32)]).reshape(1, -1)
        wts = (w_pa[l].astype(bf16), w_pb[l].astype(bf16), w_o[l].astype(bf16),
               ln1_g[l].reshape(1, -1), ln1_b[l].reshape(1, -1), wr, br)
        x1_p, xpk_p, route_p = _merge(ao_p, sg_p, gate_p, xp, wmix_p, bmix_p, *wts, alpha)
        x1_s, xpk_s, route_s = _merge(ao_s, sg_s, gate_s, xs, wmix_s, bmix_s, *wts, alpha)

        pos, tile_expert, n_active, w_slot, next_expert = _moe_plan(route_p, route_s, n_tiles)
        pos_p = pos[:n_prompt * TOP_K].reshape(n_prompt // ROW_TILE, 1, ROW_TILE * TOP_K)
        pos_s = pos[n_prompt * TOP_K:].reshape(1, 1, n_sample * TOP_K)
        xsort = _dispatch(pos_p, xpk_p, xsort)
        xsort = _dispatch(pos_s, xpk_s, xsort)
        ysort = _gmm(tile_expert, n_active, w_slot, next_expert, xsort, w_gate, w_up, w_down, l)
        g2 = ln2_g[l].reshape(1, -1)
        b2 = ln2_b[l].reshape(1, -1)
        xp, xpb = _combine(pos_p, x1_p, route_p, g2, b2, ysort, alpha)
        xs, xsb = _combine(pos_s, x1_s, route_s, g2, b2, ysort, alpha)

        ks_rows.append(k_new)
        vs_rows.append(v_new)
        sgv_rows.append(sg_s[:, SG_WIDTH:].reshape(dec_batch, dec_seq, SG_WIDTH))

    yp = xp.reshape(batch, seq, D_MODEL)
    ys = xs.reshape(dec_batch, dec_seq, D_MODEL)
    kv_shape = (depth, batch, seq, N_HEADS, KEY_DIM)
    return (yp, ys, k_stack.reshape(kv_shape), v_stack.reshape(kv_shape), jnp.stack(ks_rows),
            jnp.stack(vs_rows), jnp.stack(sgv_rows))
```

```python
import functools
import math

import jax
import jax.numpy as jnp
from jax import lax
from jax.experimental import pallas as pl
from jax.experimental.pallas import tpu as pltpu

D_MODEL = 2048
N_HEADS = 8
HEAD_DIM = 64
KEY_DIM = 2 * HEAD_DIM
VAL_DIM = 2 * HEAD_DIM
Q_W = N_HEADS * KEY_DIM
SG_WIDTH = 1024
SG_GROUPS = 8
SG_CH = SG_WIDTH // SG_GROUPS
CHUNK = 128
N_EGROUPS = 4
EXP_PER_GROUP = 8
N_EXPERTS = N_EGROUPS * EXP_PER_GROUP
TOP_K = 2
D_EXPERT = 512
LN_EPS = 1e-5
NEG = -1e30
HEAD_SHIFT = N_HEADS.bit_length() - 1
assert 1 << HEAD_SHIFT == N_HEADS
HALF_D = D_MODEL // 2
SCORE_SCALE = HEAD_DIM ** -0.5 * math.log2(math.e)
ONES_ROWS = 16

LANES = 128
COL_BLOCK = 1024
ROW_TILE = 256
PROJ_ROW_TILE = 1024
ATT_BLOCK = 512
ATT_COLS = 512
PAGES_PER_STEP = 8
NEW_TOKEN_ROWS = 16
EXPERT_TILE = 256
VMEM_LIMIT = 56 * 1024 * 1024


def _lambda_init(layer_idx):
    return 0.8 - 0.6 * math.exp(-0.3 * layer_idx)


def _layer_norm_rows(x, g, b):
    mu = jnp.mean(x, axis=-1, keepdims=True)
    xc = x - mu
    var = jnp.mean(xc * xc, axis=-1, keepdims=True)
    return xc * lax.rsqrt(var + LN_EPS) * g + b


def _lam_value(lamp_ref, lam_i):
    a = jnp.sum(lamp_ref[0:1, :] * lamp_ref[1:2, :], axis=-1, keepdims=True)
    b = jnp.sum(lamp_ref[2:3, :] * lamp_ref[3:4, :], axis=-1, keepdims=True)
    return jnp.exp(a) - jnp.exp(b) + lam_i


def _params(n_axes):
    return pltpu.CompilerParams(dimension_semantics=("arbitrary",) * n_axes, vmem_limit_bytes=VMEM_LIMIT)


def _qkv_block(x_ref, w_ref, b_ref):
    z = jnp.dot(x_ref[...], w_ref[...], preferred_element_type=jnp.float32) + b_ref[...]
    scale = jnp.where(pl.program_id(0) == 0, SCORE_SCALE, 1.0).astype(jnp.float32)
    return z * scale


def _proj_qkv_kernel(x_ref, w_ref, b_ref, o_ref):
    o_ref[...] = _qkv_block(x_ref, w_ref, b_ref)


def _proj_qkv_rows_kernel(x_ref, w_ref, b_ref, kin_ref, vin_ref, o_ref, ko_ref, vo_ref):
    del kin_ref, vin_ref
    z = _qkv_block(x_ref, w_ref, b_ref)
    o_ref[...] = z
    j = pl.program_id(0)

    def per_head(dst_ref):
        for h in range(N_HEADS):
            dst_ref[:, h, :] = z[:, h * KEY_DIM:(h + 1) * KEY_DIM]

    @pl.when(j == 1)
    def _():
        per_head(ko_ref)

    @pl.when(j == 2)
    def _():
        per_head(vo_ref)


def _proj_sg_kernel(x_ref, w_ref, b_ref, g_ref, bb_ref, o_ref):
    z = jnp.dot(x_ref[...], w_ref[...], preferred_element_type=jnp.float32) + b_ref[...]
    y = jax.nn.gelu(z)
    j = pl.program_id(0)

    @pl.when(j == 0)
    def _():
        o_ref[...] = y

    @pl.when(j == 1)
    def _():
        o_ref[...] = _layer_norm_rows(y, g_ref[...], bb_ref[...])


def _proj_gate_kernel(x_ref, w_ref, b_ref, o_ref):
    z = jnp.dot(x_ref[...], w_ref[...], preferred_element_type=jnp.float32) + b_ref[...]
    o_ref[...] = (1.0 / (1.0 + jnp.exp(-z))).astype(o_ref.dtype)


def _proj_in_specs(tm, off):
    return [
        pl.BlockSpec((tm, D_MODEL), lambda j, i: (i, 0)),
        pl.BlockSpec((D_MODEL, COL_BLOCK), lambda j, i: (0, off + j)),
        pl.BlockSpec((1, COL_BLOCK), lambda j, i: (0, off + j)),
    ]


def _proj(body, xb, w, b, col_off, n_col, out_dtype, extra=(), name=None):
    rows = xb.shape[0]
    tm = min(PROJ_ROW_TILE, rows)
    extra_specs = [pl.BlockSpec((1, COL_BLOCK), lambda j, i: (0, 0)) for _ in extra]
    return pl.pallas_call(
        body,
        grid=(n_col // COL_BLOCK, rows // tm),
        in_specs=_proj_in_specs(tm, col_off // COL_BLOCK) + extra_specs,
        out_specs=pl.BlockSpec((tm, COL_BLOCK), lambda j, i: (i, j)),
        out_shape=jax.ShapeDtypeStruct((rows, n_col), out_dtype),
        compiler_params=_params(2),
        name=name,
    )(xb, w, b, *extra)


def _proj_qkv_rows(xb, w, b, k_stack, v_stack, layer):
    rows = xb.shape[0]
    tm = PROJ_ROW_TILE
    n_i = rows // tm

    def k_rows(j, i):
        return (layer, jnp.where(j < 1, 0, jnp.where(j == 1, i, n_i - 1)), 0, 0)

    def v_rows(j, i):
        return (layer, jnp.where(j < 2, 0, i), 0, 0)

    any_spec = pl.BlockSpec(memory_space=pl.ANY)
    return pl.pallas_call(
        _proj_qkv_rows_kernel,
        grid=(3, n_i),
        in_specs=_proj_in_specs(tm, 0) + [any_spec, any_spec],
        out_specs=[pl.BlockSpec((tm, COL_BLOCK), lambda j, i: (i, j)),
                   pl.BlockSpec((None, tm, N_HEADS, KEY_DIM), k_rows),
                   pl.BlockSpec((None, tm, N_HEADS, VAL_DIM), v_rows)],
        out_shape=[jax.ShapeDtypeStruct((rows, 3 * Q_W), jnp.float32),
                   jax.ShapeDtypeStruct(k_stack.shape, k_stack.dtype),
                   jax.ShapeDtypeStruct(v_stack.shape, v_stack.dtype)],
        input_output_aliases={3: 1, 4: 2},
        compiler_params=_params(2),
        name="proj_qkv_rows",
    )(xb, w, b, k_stack, v_stack)


def _flash_kernel(q_ref, k_ref, v_ref, lamp_ref, g_ref, o_ref,
                  kb_ref, vt_ref, qs_ref, m_ref, acc_ref, *, lam_i, n_blk):
    tq = ATT_BLOCK
    qi = pl.program_id(1)

    @pl.when(qi == 0)
    def _():
        for c in range(n_blk):
            rows = slice(c * tq, (c + 1) * tq)
            kb_ref[c] = k_ref[rows, :].astype(jnp.bfloat16)
            vt_ref[c, :VAL_DIM, :] = v_ref[rows, :].T.astype(jnp.bfloat16)
            vt_ref[c, VAL_DIM:, :] = jnp.ones((ONES_ROWS, tq), jnp.bfloat16)

    qt = q_ref[...].T
    row = lax.broadcasted_iota(jnp.int32, qt.shape, 0)
    qs_ref[:, :tq] = jnp.where(row < HEAD_DIM, qt, 0.0).astype(jnp.bfloat16)
    qs_ref[:, tq:] = jnp.where(row >= HEAD_DIM, qt, 0.0).astype(jnp.bfloat16)
    m_ref[...] = jnp.full(m_ref.shape, NEG, jnp.float32)
    acc_ref[...] = jnp.zeros(acc_ref.shape, jnp.float32)

    def step(ki, masked):
        kblk = kb_ref[ki]
        vblk = vt_ref[ki]
        n_chains = 2 * tq // ATT_COLS

        def scores(c):
            cols = slice(c * ATT_COLS, (c + 1) * ATT_COLS)
            s = jnp.dot(kblk, qs_ref[:, cols], preferred_element_type=jnp.float32)
            if masked:
                kpos = lax.broadcasted_iota(jnp.int32, s.shape, 0)
                qpos = lax.broadcasted_iota(jnp.int32, s.shape, 1) + (c * ATT_COLS) % tq
                s = jnp.where(kpos <= qpos, s, NEG)
            return s

        new_state = []
        s_next = scores(0)
        for c in range(n_chains):
            cols = slice(c * ATT_COLS, (c + 1) * ATT_COLS)
            s = s_next
            if c + 1 < n_chains:
                s_next = scores(c + 1)
            m_prev = m_ref[:, cols]
            m_cur = jnp.maximum(m_prev, jnp.max(s, axis=0, keepdims=True))
            alpha = jnp.exp2(m_prev - m_cur)
            p = jnp.exp2(s - m_cur)
            acc_new = alpha * acc_ref[:, cols] + jnp.dot(
                vblk, p.astype(jnp.bfloat16), preferred_element_type=jnp.float32)
            new_state.append((cols, m_cur, acc_new))
        for cols, m_cur, acc_new in new_state:
            m_ref[:, cols] = m_cur
            acc_ref[:, cols] = acc_new

    def body(ki, carry):
        step(ki, False)
        return carry

    lax.fori_loop(0, qi, body, 0)
    step(qi, True)

    lam = _lam_value(lamp_ref, lam_i)
    l_row = acc_ref[VAL_DIM:VAL_DIM + 1, :]
    o = (acc_ref[:VAL_DIM, :tq] / l_row[:, :tq]) - lam * (acc_ref[:VAL_DIM, tq:] / l_row[:, tq:])
    ms = jnp.mean(o * o, axis=0, keepdims=True)
    r = o * lax.rsqrt(ms + LN_EPS) * g_ref[...] * (1.0 - lam_i)
    o_ref[...] = r.T.astype(o_ref.dtype)


def _flash(qkv, lamp, g_col, lam_i, batch, seq):
    tq = ATT_BLOCK
    n_blk = seq // tq
    kcol = Q_W // KEY_DIM
    vcol = 2 * Q_W // KEY_DIM
    return pl.pallas_call(
        functools.partial(_flash_kernel, lam_i=lam_i, n_blk=n_blk),
        grid=(batch * N_HEADS, n_blk),
        in_specs=[
            pl.BlockSpec((tq, KEY_DIM), lambda bh, qi: ((bh // N_HEADS) * n_blk + qi, bh % N_HEADS)),
            pl.BlockSpec((seq, KEY_DIM), lambda bh, qi: (bh // N_HEADS, kcol + bh % N_HEADS)),
            pl.BlockSpec((seq, VAL_DIM), lambda bh, qi: (bh // N_HEADS, vcol + bh % N_HEADS)),
            pl.BlockSpec((4, HEAD_DIM), lambda bh, qi: (0, 0)),
            pl.BlockSpec((VAL_DIM, 1), lambda bh, qi: (0, 0)),
        ],
        out_specs=pl.BlockSpec((tq, VAL_DIM), lambda bh, qi: ((bh // N_HEADS) * n_blk + qi, bh % N_HEADS)),
        out_shape=jax.ShapeDtypeStruct((batch * seq, N_HEADS * VAL_DIM), jnp.bfloat16),
        scratch_shapes=[
            pltpu.VMEM((n_blk, tq, KEY_DIM), jnp.bfloat16),
            pltpu.VMEM((n_blk, VAL_DIM + ONES_ROWS, tq), jnp.bfloat16),
            pltpu.VMEM((KEY_DIM, 2 * tq), jnp.bfloat16),
            pltpu.VMEM((1, 2 * tq), jnp.float32),
            pltpu.VMEM((VAL_DIM + ONES_ROWS, 2 * tq), jnp.float32),
        ],
        compiler_params=_params(2),
        name="flash_diff_attn",
    )(qkv, qkv, qkv, lamp, g_col)


def _decode_kernel(pt_ref, qbig_ref, *refs, lam_i, dec_seq):
    del pt_ref
    pp = PAGES_PER_STEP
    k_refs = refs[:pp]
    v_refs = refs[pp:2 * pp]
    kn_ref, vn_ref, lamp_ref, g_ref, o_ref, m_ref, l_ref, acc_ref = refs[2 * pp:]
    p_id = pl.program_id(1)
    nq = qbig_ref.shape[0]
    half = nq // 2

    @pl.when(p_id == 0)
    def _():
        m_ref[...] = jnp.full(m_ref.shape, NEG, jnp.float32)
        l_ref[...] = jnp.zeros(l_ref.shape, jnp.float32)
        acc_ref[...] = jnp.zeros(acc_ref.shape, jnp.float32)

    def flat(refs_, width):
        parts = [r[...].reshape(r.shape[0] * N_HEADS, width) for r in refs_]
        rows = parts[0] if len(parts) == 1 else jnp.concatenate(parts, axis=0)
        return rows.astype(jnp.bfloat16)

    def update(k2, v2, causal):
        s = lax.dot_general(qbig_ref[...], k2, (((1,), (1,)), ((), ())),
                            preferred_element_type=jnp.float32)
        qrow = lax.broadcasted_iota(jnp.int32, s.shape, 0)
        kcol = lax.broadcasted_iota(jnp.int32, s.shape, 1)
        ok = (qrow & (N_HEADS - 1)) == (kcol & (N_HEADS - 1))
        if causal:
            q_t = (qrow >> HEAD_SHIFT) & (dec_seq - 1)
            ok = ok & ((kcol >> HEAD_SHIFT) <= q_t)
        s = jnp.where(ok, s, NEG)
        m_prev = m_ref[...]
        m_cur = jnp.maximum(m_prev, jnp.max(s, axis=-1, keepdims=True))
        alpha = jnp.exp2(m_prev - m_cur)
        p = jnp.exp2(s - m_cur)
        l_ref[...] = alpha * l_ref[...] + jnp.sum(p, axis=-1, keepdims=True)
        acc_ref[...] = alpha * acc_ref[...] + jnp.dot(
            p.astype(jnp.bfloat16), v2, preferred_element_type=jnp.float32)
        m_ref[...] = m_cur

    update(flat(k_refs, KEY_DIM), flat(v_refs, VAL_DIM), False)

    @pl.when(p_id == pl.num_programs(1) - 1)
    def _():
        update(flat([kn_ref], KEY_DIM), flat([vn_ref], VAL_DIM), True)
        lam = _lam_value(lamp_ref, lam_i)
        o1 = acc_ref[0:half, :] / l_ref[0:half, :]
        o2 = acc_ref[half:nq, :] / l_ref[half:nq, :]
        d = o1 - lam * o2
        ms = jnp.mean(d * d, axis=-1, keepdims=True)
        o_ref[...] = d * lax.rsqrt(ms + LN_EPS) * g_ref[...] * (1.0 - lam_i)


def _decode(page_table, qbig, cache_k, cache_v, layer, kn, vn, lamp, g_row, lam_i, dec_seq):
    dec_batch, n_pages = page_table.shape
    pp = PAGES_PER_STEP
    page = cache_k.shape[2]
    nq = qbig.shape[1]
    new_rows = kn.shape[1]

    def page_spec(r):
        return pl.BlockSpec((None, None, page, N_HEADS, KEY_DIM),
                            lambda b, p, pt: (layer, pt[b, p * pp + r], 0, 0, 0))

    grid_spec = pltpu.PrefetchScalarGridSpec(
        num_scalar_prefetch=1,
        grid=(dec_batch, n_pages // pp),
        in_specs=[pl.BlockSpec((None, nq, KEY_DIM), lambda b, p, pt: (b, 0, 0))]
        + [page_spec(r) for r in range(pp)] + [page_spec(r) for r in range(pp)]
        + [pl.BlockSpec((None, new_rows, N_HEADS, KEY_DIM), lambda b, p, pt: (b, 0, 0, 0)),
           pl.BlockSpec((None, new_rows, N_HEADS, VAL_DIM), lambda b, p, pt: (b, 0, 0, 0)),
           pl.BlockSpec((4, HEAD_DIM), lambda b, p, pt: (0, 0)),
           pl.BlockSpec((1, VAL_DIM), lambda b, p, pt: (0, 0))],
        out_specs=pl.BlockSpec((None, nq // 2, VAL_DIM), lambda b, p, pt: (b, 0, 0)),
        scratch_shapes=[
            pltpu.VMEM((nq, 1), jnp.float32),
            pltpu.VMEM((nq, 1), jnp.float32),
            pltpu.VMEM((nq, VAL_DIM), jnp.float32),
        ],
    )
    return pl.pallas_call(
        functools.partial(_decode_kernel, lam_i=lam_i, dec_seq=dec_seq),
        grid_spec=grid_spec,
        out_shape=jax.ShapeDtypeStruct((dec_batch, nq // 2, VAL_DIM), jnp.float32),
        compiler_params=_params(2),
        name="paged_diff_attn",
    )(page_table, qbig, *([cache_k] * pp), *([cache_v] * pp), kn, vn, lamp, g_row)


def _merge_kernel(ao_ref, sg_ref, gate_ref, x_ref, wmix_ref, bmix_ref, wpa_ref, wpb_ref, wo_ref,
                  g1_ref, b1_ref, wr_ref, br_ref, x1_ref, xpk_ref, route_ref, so_ref, *, alpha):
    tm = x_ref.shape[0]
    ch = wmix_ref.shape[-1]
    for c in range(tm // ch):
        rows = slice(c * ch, (c + 1) * ch)
        for g in range(SG_GROUPS):
            cols = slice(g * SG_CH, (g + 1) * SG_CH)
            vs = sg_ref[rows, SG_WIDTH + g * SG_CH:SG_WIDTH + (g + 1) * SG_CH].astype(jnp.bfloat16)
            mixed = jnp.dot(wmix_ref[g], vs, preferred_element_type=jnp.float32) + bmix_ref[:, cols]
            so_ref[rows, cols] = (sg_ref[rows, cols] * mixed).astype(jnp.bfloat16)

    a = jnp.dot(ao_ref[...], wpa_ref[...], preferred_element_type=jnp.float32)
    bm = jnp.dot(so_ref[...], wpb_ref[...], preferred_element_type=jnp.float32)
    merged = gate_ref[:, :D_MODEL].astype(jnp.float32) * a + gate_ref[:, D_MODEL:].astype(jnp.float32) * bm
    y = jnp.dot(merged.astype(jnp.bfloat16), wo_ref[...], preferred_element_type=jnp.float32)
    x1 = _layer_norm_rows(alpha * x_ref[...] + y, g1_ref[...], b1_ref[...])
    x1_ref[...] = x1

    x1b = x1.astype(jnp.bfloat16)
    lo = lax.bitcast_convert_type(x1b[:, :HALF_D].astype(jnp.float32), jnp.uint32)
    hi = lax.bitcast_convert_type(x1b[:, HALF_D:].astype(jnp.float32), jnp.uint32)
    xpk_ref[...] = (lo >> 16) | (hi & jnp.uint32(0xFFFF0000))

    logits = jnp.dot(x1b, wr_ref[...], preferred_element_type=jnp.float32) + br_ref[...]
    lane = lax.broadcasted_iota(jnp.int32, logits.shape, 1)
    ninf = -jnp.inf
    gmask = lane < N_EGROUPS
    gl = jnp.where(gmask, logits, ninf)
    gmax = jnp.max(gl, axis=-1, keepdims=True)
    gi = jnp.min(jnp.where(gl == gmax, lane, LANES), axis=-1, keepdims=True)
    gsum = jnp.sum(jnp.where(gmask, jnp.exp(gl - gmax), 0.0), axis=-1, keepdims=True)
    gw = 1.0 / gsum
    emask = (lane >= N_EGROUPS) & (lane < N_EGROUPS + N_EXPERTS) & (((lane - N_EGROUPS) // EXP_PER_GROUP) == gi)
    el = jnp.where(emask, logits, ninf)
    v1 = jnp.max(el, axis=-1, keepdims=True)
    i1 = jnp.min(jnp.where(el == v1, lane, LANES), axis=-1, keepdims=True)
    el2 = jnp.where(lane == i1, ninf, el)
    v2 = jnp.max(el2, axis=-1, keepdims=True)
    i2 = jnp.min(jnp.where(el2 == v2, lane, LANES), axis=-1, keepdims=True)
    e2 = jnp.exp(v2 - v1)
    den = 1.0 + e2
    tw1 = (1.0 / den) * gw
    tw2 = (e2 / den) * gw
    id1 = (i1 - N_EGROUPS).astype(jnp.float32)
    id2 = (i2 - N_EGROUPS).astype(jnp.float32)
    route_ref[...] = jnp.where(lane == 0, id1, jnp.where(lane == 1, id2,
                               jnp.where(lane == 2, tw1, jnp.where(lane == 3, tw2, 0.0))))


def _merge(ao, sg, gate, x, wmix, bmix, wpa, wpb, wo, g1, b1, wr, br, alpha):
    rows = x.shape[0]
    tm = min(ROW_TILE, rows)

    def full(arr):
        nd = arr.ndim
        return pl.BlockSpec(arr.shape, lambda i, _nd=nd: (0,) * _nd, pipeline_mode=pl.Buffered(1))

    return pl.pallas_call(
        functools.partial(_merge_kernel, alpha=alpha),
        grid=(rows // tm,),
        in_specs=[
            pl.BlockSpec((tm, ao.shape[1]), lambda i: (i, 0)),
            pl.BlockSpec((tm, sg.shape[1]), lambda i: (i, 0)),
            pl.BlockSpec((tm, gate.shape[1]), lambda i: (i, 0)),
            pl.BlockSpec((tm, D_MODEL), lambda i: (i, 0)),
            full(wmix), full(bmix), full(wpa), full(wpb), full(wo), full(g1), full(b1), full(wr), full(br),
        ],
        out_specs=[pl.BlockSpec((tm, D_MODEL), lambda i: (i, 0)),
                   pl.BlockSpec((tm, HALF_D), lambda i: (i, 0)),
                   pl.BlockSpec((tm, LANES), lambda i: (i, 0))],
        out_shape=[jax.ShapeDtypeStruct((rows, D_MODEL), jnp.float32),
                   jax.ShapeDtypeStruct((rows, HALF_D), jnp.uint32),
                   jax.ShapeDtypeStruct((rows, LANES), jnp.float32)],
        scratch_shapes=[pltpu.VMEM((tm, SG_WIDTH), jnp.bfloat16)],
        compiler_params=_params(1),
        name="merge_ln_router",
    )(ao, sg, gate, x, wmix, bmix, wpa, wpb, wo, g1, b1, wr, br)


def _moe_plan(route_p, route_s, n_tiles):
    te = EXPERT_TILE
    ef = jnp.concatenate([route_p[:, :TOP_K], route_s[:, :TOP_K]], axis=0).astype(jnp.int32).reshape(-1)
    onehot = (ef[:, None] == jnp.arange(N_EXPERTS, dtype=jnp.int32)[None, :]).astype(jnp.int32)
    csum = jnp.cumsum(onehot, axis=0)
    counts = csum[-1]
    rank = jnp.sum(csum * onehot, axis=1) - 1
    pcounts = ((counts + te - 1) // te) * te
    pend = jnp.cumsum(pcounts)
    pstart = pend - pcounts
    pos = jnp.sum(onehot * pstart[None, :], axis=1) + rank
    n_active = (pend[-1] // te).astype(jnp.int32).reshape(1)
    tile_start = jnp.minimum(jnp.arange(n_tiles, dtype=jnp.int32) * te, pend[-1] - 1)
    tile_expert = jnp.minimum(jnp.sum((tile_start[:, None] >= pend[None, :]).astype(jnp.int32), axis=1),
                              N_EXPERTS - 1).astype(jnp.int32)
    eids = jnp.arange(N_EXPERTS, dtype=jnp.int32)[None, :]
    present = (counts > 0)[None, :]
    slot = jnp.sum((present & (eids < tile_expert[:, None])).astype(jnp.int32), axis=1) % 2
    later = jnp.min(jnp.where(present & (eids > tile_expert[:, None]), eids, N_EXPERTS), axis=1)
    next_expert = jnp.where(later < N_EXPERTS, later, -1).astype(jnp.int32)
    return pos.astype(jnp.int32), tile_expert, n_active, slot.astype(jnp.int32), next_expert


def _dispatch_kernel(pos_ref, x_ref, xs_in, xs_hbm, xbuf, sems):
    del xs_in
    tt = x_ref.shape[0]
    i = pl.program_id(0)
    slot = i % 2

    def wait_tile(s):
        for _ in range(TOP_K):
            pltpu.make_async_copy(xbuf.at[s], xs_hbm.at[pl.ds(0, tt)], sems.at[s]).wait()

    @pl.when(i >= 2)
    def _():
        wait_tile(slot)

    xbuf[slot] = x_ref[...]

    def body(r, c):
        for k in range(TOP_K):
            pltpu.make_async_copy(xbuf.at[slot, pl.ds(r, 1)],
                                  xs_hbm.at[pl.ds(pos_ref[0, 0, r * TOP_K + k], 1)],
                                  sems.at[slot]).start(priority=k % 2)
        return c

    lax.fori_loop(0, tt, body, 0, unroll=True)

    n = pl.num_programs(0)

    @pl.when(i == n - 1)
    def _():
        wait_tile(slot)

        @pl.when(n >= 2)
        def _():
            wait_tile(1 - slot)


def _dispatch(pos3, xpk, xsort):
    n_tok_tiles = pos3.shape[0]
    tt = pos3.shape[2] // TOP_K
    return pl.pallas_call(
        _dispatch_kernel,
        grid=(n_tok_tiles,),
        in_specs=[pl.BlockSpec((1, 1, pos3.shape[2]), lambda i: (i, 0, 0), memory_space=pltpu.SMEM),
                  pl.BlockSpec((tt, HALF_D), lambda i: (i, 0)),
                  pl.BlockSpec(memory_space=pl.ANY)],
        out_specs=pl.BlockSpec(memory_space=pl.ANY),
        out_shape=jax.ShapeDtypeStruct(xsort.shape, xsort.dtype),
        scratch_shapes=[pltpu.VMEM((2, tt, HALF_D), jnp.uint32), pltpu.SemaphoreType.DMA((2,))],
        input_output_aliases={2: 0},
        compiler_params=_params(1),
        name="moe_dispatch",
    )(pos3, xpk, xsort)


def _gmm_kernel(te_ref, na_ref, slot_ref, next_ref, xs_ref, wg_hbm, wu_hbm, wd_hbm, y_ref,
                wgf_ref, wuf_ref, wdf_ref, wgb_ref, wub_ref, wdb_ref, sems, *, layer):
    t = pl.program_id(0)
    n_active = na_ref[0]

    def fetch(e, s):
        return (pltpu.make_async_copy(wg_hbm.at[layer, e], wgf_ref.at[s], sems.at[s, 0]),
                pltpu.make_async_copy(wu_hbm.at[layer, e], wuf_ref.at[s], sems.at[s, 1]),
                pltpu.make_async_copy(wd_hbm.at[layer, e], wdf_ref.at[s], sems.at[s, 2]))

    @pl.when((t == 0) & (n_active > 0))
    def _():
        for c in fetch(te_ref[0], 0):
            c.start()

    @pl.when(t < n_active)
    def _():
        prev = te_ref[jnp.maximum(t - 1, 0)]

        @pl.when((t == 0) | (te_ref[t] != prev))
        def _():
            s = slot_ref[t]
            for c in fetch(te_ref[t], s):
                c.wait()
            wgb_ref[...] = wgf_ref[s].astype(jnp.bfloat16)
            wub_ref[...] = wuf_ref[s].astype(jnp.bfloat16)
            wdb_ref[...] = wdf_ref[s].astype(jnp.bfloat16)
            nxt = next_ref[t]

            @pl.when(nxt >= 0)
            def _():
                for c in fetch(nxt, 1 - s):
                    c.start()

        u = xs_ref[...]
        lo = lax.bitcast_convert_type(u << 16, jnp.float32).astype(jnp.bfloat16)
        hi = lax.bitcast_convert_type(u & jnp.uint32(0xFFFF0000), jnp.float32).astype(jnp.bfloat16)

        def up(w_ref):
            return (jnp.dot(lo, w_ref[:HALF_D, :], preferred_element_type=jnp.float32)
                    + jnp.dot(hi, w_ref[HALF_D:, :], preferred_element_type=jnp.float32))

        hg = up(wgb_ref)
        hu = up(wub_ref)
        act = (hg * (1.0 / (1.0 + jnp.exp(-hg))) * hu).astype(jnp.bfloat16)
        y_ref[...] = jnp.dot(act, wdb_ref[...], preferred_element_type=jnp.float32)

    @pl.when(t >= n_active)
    def _():
        y_ref[...] = jnp.zeros(y_ref.shape, jnp.float32)


def _gmm(tile_expert, n_active, slot, next_expert, xsort, w_gate, w_up, w_down, layer):
    te = EXPERT_TILE
    n_tiles = xsort.shape[0] // te
    any_spec = pl.BlockSpec(memory_space=pl.ANY)
    grid_spec = pltpu.PrefetchScalarGridSpec(
        num_scalar_prefetch=4,
        grid=(n_tiles,),
        in_specs=[pl.BlockSpec((te, HALF_D), lambda t, *_: (t, 0)), any_spec, any_spec, any_spec],
        out_specs=pl.BlockSpec((te, D_MODEL), lambda t, *_: (t, 0)),
        scratch_shapes=[
            pltpu.VMEM((2, D_MODEL, D_EXPERT), jnp.float32),
            pltpu.VMEM((2, D_MODEL, D_EXPERT), jnp.float32),
            pltpu.VMEM((2, D_EXPERT, D_MODEL), jnp.float32),
            pltpu.VMEM((D_MODEL, D_EXPERT), jnp.bfloat16),
            pltpu.VMEM((D_MODEL, D_EXPERT), jnp.bfloat16),
            pltpu.VMEM((D_EXPERT, D_MODEL), jnp.bfloat16),
            pltpu.SemaphoreType.DMA((2, 3)),
        ],
    )
    return pl.pallas_call(
        functools.partial(_gmm_kernel, layer=layer),
        grid_spec=grid_spec,
        out_shape=jax.ShapeDtypeStruct((xsort.shape[0], D_MODEL), jnp.float32),
        compiler_params=_params(1),
        name="moe_grouped",
    )(tile_expert, n_active, slot, next_expert, xsort, w_gate, w_up, w_down)


def _combine_kernel(pos_ref, posn_ref, x1_ref, route_ref, g_ref, b_ref, y_hbm, x2_ref, x2b_ref,
                    gbuf, sem, *, alpha):
    tt = x1_ref.shape[0]
    i = pl.program_id(0)
    slot = i % 2

    def start(idx_ref, s_):
        def body(r, c):
            for k in range(TOP_K):
                pltpu.make_async_copy(y_hbm.at[pl.ds(idx_ref[0, 0, r * TOP_K + k], 1)],
                                      gbuf.at[s_, k, pl.ds(r, 1)], sem.at[s_]).start()
            return c
        lax.fori_loop(0, tt, body, 0, unroll=True)

    @pl.when(i == 0)
    def _():
        start(pos_ref, 0)

    for k in range(TOP_K):
        pltpu.make_async_copy(y_hbm.at[pl.ds(0, tt)], gbuf.at[slot, k], sem.at[slot]).wait()

    @pl.when(i + 1 < pl.num_programs(0))
    def _():
        start(posn_ref, 1 - slot)

    ff = (route_ref[:, TOP_K:TOP_K + 1] * gbuf[slot, 0]
          + route_ref[:, TOP_K + 1:TOP_K + 2] * gbuf[slot, 1])
    x2 = _layer_norm_rows(alpha * x1_ref[...] + ff, g_ref[...], b_ref[...])
    x2_ref[...] = x2
    x2b_ref[...] = x2.astype(jnp.bfloat16)


def _combine(pos3, x1, route, g2, b2, ysort, alpha):
    rows = x1.shape[0]
    tt = pos3.shape[2] // TOP_K
    n_i = rows // tt
    return pl.pallas_call(
        functools.partial(_combine_kernel, alpha=alpha),
        grid=(n_i,),
        in_specs=[
            pl.BlockSpec((1, 1, tt * TOP_K), lambda i: (i, 0, 0), memory_space=pltpu.SMEM),
            pl.BlockSpec((1, 1, tt * TOP_K), lambda i: (jnp.minimum(i + 1, n_i - 1), 0, 0),
                         memory_space=pltpu.SMEM),
            pl.BlockSpec((tt, D_MODEL), lambda i: (i, 0)),
            pl.BlockSpec((tt, LANES), lambda i: (i, 0)),
            pl.BlockSpec((1, D_MODEL), lambda i: (0, 0)),
            pl.BlockSpec((1, D_MODEL), lambda i: (0, 0)),
            pl.BlockSpec(memory_space=pl.ANY),
        ],
        out_specs=[pl.BlockSpec((tt, D_MODEL), lambda i: (i, 0)),
                   pl.BlockSpec((tt, D_MODEL), lambda i: (i, 0))],
        out_shape=[jax.ShapeDtypeStruct((rows, D_MODEL), jnp.float32),
                   jax.ShapeDtypeStruct((rows, D_MODEL), jnp.bfloat16)],
        scratch_shapes=[pltpu.VMEM((2, TOP_K, tt, D_MODEL), jnp.float32),
                        pltpu.SemaphoreType.DMA((2,))],
        compiler_params=_params(1),
        name="combine_ln",
    )(pos3, pos3, x1, route, g2, b2, ysort)


def kernel(x_prompt, x_sample, cache_k, cache_v, page_table, w_in, b_in, lam_q1, lam_k1, lam_q2, lam_k2, subln_g, sg_ln_g, sg_ln_b, w_sp, b_sp, w_pa, w_pb, w_o, ln1_g, ln1_b, w_rg, b_rg, w_re, b_re, w_gate, w_up, w_down, ln2_g, ln2_b):
    batch, seq, _ = x_prompt.shape
    dec_batch, dec_seq, _ = x_sample.shape
    depth = w_in.shape[0]
    n_prompt = batch * seq
    n_sample = dec_batch * dec_seq
    assert seq % ATT_BLOCK == 0 and n_prompt % PROJ_ROW_TILE == 0 and seq % CHUNK == 0
    assert n_sample <= CHUNK and n_sample % 16 == 0 and page_table.shape[1] % PAGES_PER_STEP == 0
    assert dec_seq & (dec_seq - 1) == 0 and dec_seq <= NEW_TOKEN_ROWS
    alpha = (2 * depth) ** 0.25
    bf16 = jnp.bfloat16
    f32 = jnp.float32
    te = EXPERT_TILE
    n_assign = (n_prompt + n_sample) * TOP_K
    n_tiles = -(-n_assign // te) + N_EXPERTS

    xp = x_prompt.reshape(n_prompt, D_MODEL)
    xs = x_sample.reshape(n_sample, D_MODEL)
    xpb = xp.astype(bf16)
    xsb = xs.astype(bf16)
    k_stack = jnp.zeros((depth, n_prompt, N_HEADS, KEY_DIM), f32)
    v_stack = jnp.zeros((depth, n_prompt, N_HEADS, VAL_DIM), f32)
    xsort = jnp.zeros((n_tiles * te, HALF_D), jnp.uint32)

    ridx = jnp.arange(n_sample)
    rt = ridx % dec_seq
    rs = ridx // dec_seq
    smask = ((rt[None, :] <= rt[:, None]) & (rs[None, :] == rs[:, None])).astype(f32)
    tril = jnp.tril(jnp.ones((CHUNK, CHUNK), f32))
    eye_m = jnp.eye(2, dtype=f32)

    ks_rows, vs_rows, sgv_rows = [], [], []
    for l in range(depth):
        lam_i = _lambda_init(l)
        w_in_b = w_in[l].astype(bf16)
        b_row = b_in[l].reshape(1, -1)
        sg_g = sg_ln_g[l].reshape(1, -1)
        sg_b = sg_ln_b[l].reshape(1, -1)
        sg_off = 3 * Q_W
        gate_off = sg_off + 2 * SG_WIDTH
        qkv_p, k_stack, v_stack = _proj_qkv_rows(xpb, w_in_b, b_row, k_stack, v_stack, l)
        qkv_s = _proj(_proj_qkv_kernel, xsb, w_in_b, b_row, 0, 3 * Q_W, f32, name="proj_qkv_s")
        sg_p = _proj(_proj_sg_kernel, xpb, w_in_b, b_row, sg_off, 2 * SG_WIDTH, f32, (sg_g, sg_b), "proj_sg")
        sg_s = _proj(_proj_sg_kernel, xsb, w_in_b, b_row, sg_off, 2 * SG_WIDTH, f32, (sg_g, sg_b), "proj_sg_s")
        gate_p = _proj(_proj_gate_kernel, xpb, w_in_b, b_row, gate_off, 2 * D_MODEL, bf16, name="proj_gate")
        gate_s = _proj(_proj_gate_kernel, xsb, w_in_b, b_row, gate_off, 2 * D_MODEL, bf16, name="proj_gate_s")

        lamp = jnp.stack([lam_q1[l], lam_k1[l], lam_q2[l], lam_k2[l]]).astype(f32)
        ao_p = _flash(qkv_p, lamp, subln_g[l].reshape(VAL_DIM, 1), lam_i, batch, seq)

        q5 = qkv_s[:, :Q_W].reshape(dec_batch, dec_seq, N_HEADS, 2, HEAD_DIM)
        qbig = jnp.einsum("bthmd,mn->bmthnd", q5, eye_m).reshape(
            dec_batch, 2 * dec_seq * N_HEADS, KEY_DIM).astype(bf16)
        k_new = qkv_s[:, Q_W:2 * Q_W].reshape(dec_batch, dec_seq, N_HEADS, KEY_DIM)
        v_new = qkv_s[:, 2 * Q_W:].reshape(dec_batch, dec_seq, N_HEADS, VAL_DIM)
        pad_new = ((0, 0), (0, NEW_TOKEN_ROWS - dec_seq), (0, 0), (0, 0))
        ao_s = _decode(page_table, qbig, cache_k, cache_v, l, jnp.pad(k_new, pad_new), jnp.pad(v_new, pad_new),
                       lamp, subln_g[l].reshape(1, -1), lam_i, dec_seq)
        ao_s = ao_s.reshape(n_sample, N_HEADS * VAL_DIM).astype(bf16)

        wmix_p = (w_sp[l] * tril).astype(bf16)
        wmix_s = (w_sp[l][:, rt[:, None], rt[None, :]] * smask).astype(bf16)
        bmix_p = jnp.repeat(b_sp[l].T, SG_CH, axis=1)
        bmix_s = bmix_p[rt]
        wr = jnp.concatenate([w_rg[l], w_re[l],
                              jnp.zeros((D_MODEL, LANES - N_EGROUPS - N_EXPERTS), f32)], axis=1).astype(bf16)
        br = jnp.concatenate([b_rg[l], b_re[l], jnp.zeros((LANES - N_EGROUPS - N_EXPERTS,), f32)]).reshape(1, -1)
        wts = (w_pa[l].astype(bf16), w_pb[l].astype(bf16), w_o[l].astype(bf16),
               ln1_g[l].reshape(1, -1), ln1_b[l].reshape(1, -1), wr, br)
        x1_p, xpk_p, route_p = _merge(ao_p, sg_p, gate_p, xp, wmix_p, bmix_p, *wts, alpha)
        x1_s, xpk_s, route_s = _merge(ao_s, sg_s, gate_s, xs, wmix_s, bmix_s, *wts, alpha)

        pos, tile_expert, n_active, w_slot, next_expert = _moe_plan(route_p, route_s, n_tiles)
        pos_p = pos[:n_prompt * TOP_K].reshape(n_prompt // ROW_TILE, 1, ROW_TILE * TOP_K)
        pos_s = pos[n_prompt * TOP_K:].reshape(1, 1, n_sample * TOP_K)
        xsort = _dispatch(pos_p, xpk_p, xsort)
        xsort = _dispatch(pos_s, xpk_s, xsort)
        ysort = _gmm(tile_expert, n_active, w_slot, next_expert, xsort, w_gate, w_up, w_down, l)
        g2 = ln2_g[l].reshape(1, -1)
        b2 = ln2_b[l].reshape(1, -1)
        xp, xpb = _combine(pos_p, x1_p, route_p, g2, b2, ysort, alpha)
        xs, xsb = _combine(pos_s, x1_s, route_s, g2, b2, ysort, alpha)

        ks_rows.append(k_new)
        vs_rows.append(v_new)
        sgv_rows.append(sg_s[:, SG_WIDTH:].reshape(dec_batch, dec_seq, SG_WIDTH))

    yp = xp.reshape(batch, seq, D_MODEL)
    ys = xs.reshape(dec_batch, dec_seq, D_MODEL)
    kv_shape = (depth, batch, seq, N_HEADS, KEY_DIM)
    return (yp, ys, k_stack.reshape(kv_shape), v_stack.reshape(kv_shape), jnp.stack(ks_rows),
            jnp.stack(vs_rows), jnp.stack(sgv_rows))
```
